```python
import jax, jax.numpy as jnp
from jax import lax
import numpy as np

D_MODEL = 1024
BATCH = 16
SEQ = 4096
DEPTH = 4

GRID_W = 64
MIX_W = D_MODEL
ATTN_W = MIX_W // 2
CONV_W = MIX_W - ATTN_W
HEAD_DIM = 64
N_ATTN_HEADS = ATTN_W // HEAD_DIM
N_CONV_GROUPS = CONV_W // HEAD_DIM
WIN_H_MAX = 8
WIN_W = 16
CONV_K = 3
IN_COLS = 3 * ATTN_W + 3 * CONV_W
D_FF = ((8 * D_MODEL + 3 * 256 - 1) // (3 * 256)) * 256
PLE_DIM = 256
EPS = 1e-6

kernel_name = 'hybrid_natten_shortconv_encoder'


def rmsnorm(x, g):
    xf = x.astype(jnp.float32)
    y = xf * lax.rsqrt(jnp.mean(xf * xf, axis=-1, keepdims=True) + EPS)
    return (y * g.astype(jnp.float32)).astype(x.dtype)


def neighbourhood_attention(q, k, v, rpb):
    b, s, h, dh = q.shape
    rows = s // GRID_W
    kh = min(WIN_H_MAX, rows)
    qg = q.reshape(b, rows, GRID_W, h, dh)
    kg = k.reshape(b, rows, GRID_W, h, dh)
    vg = v.reshape(b, rows, GRID_W, h, dh)
    cols = jnp.arange(GRID_W)
    col_start = jnp.clip(cols - WIN_W // 2, 0, GRID_W - WIN_W)
    col_idx = col_start[:, None] + jnp.arange(WIN_W)[None, :]
    col_off = col_idx - cols[:, None] + (WIN_W - 1)
    rpb_cols = rpb[:, :, col_off]
    scale = dh ** -0.5

    def row_block(r):
        r0 = jnp.clip(r - kh // 2, 0, rows - kh)
        q_r = lax.dynamic_index_in_dim(qg, r, axis=1, keepdims=False)
        k_r = lax.dynamic_slice_in_dim(kg, r0, kh, axis=1)
        v_r = lax.dynamic_slice_in_dim(vg, r0, kh, axis=1)
        k_nb = k_r[:, :, col_idx]
        v_nb = v_r[:, :, col_idx]
        row_off = r0 + jnp.arange(kh) - r + (WIN_H_MAX - 1)
        bias = jnp.take(rpb_cols, row_off, axis=1).transpose(0, 2, 1, 3)
        sc = jnp.einsum('bqhd,bkqjhd->bhqkj', q_r, k_nb).astype(jnp.float32) * scale + bias[None].astype(jnp.float32)
        pr = jax.nn.softmax(sc.reshape(b, h, GRID_W, kh * WIN_W), axis=-1)
        pr = pr.reshape(b, h, GRID_W, kh, WIN_W).astype(v.dtype)
        return jnp.einsum('bhqkj,bkqjhd->bqhd', pr, v_nb)

    out = lax.map(row_block, jnp.arange(rows))
    return out.transpose(1, 0, 2, 3, 4).reshape(b, s, h * dh)


def centred_depthwise_conv(u, w):
    c = u.shape[-1]
    return lax.conv_general_dilated(
        u, w[:, None, :].astype(u.dtype), window_strides=(1,),
        padding=((CONV_K // 2, CONV_K // 2),),
        dimension_numbers=('NWC', 'WIO', 'NWC'), feature_group_count=c)


def setup_inputs(seed: int = 0) -> dict:
    key = jax.random.key(seed)
    ks = jax.random.split(key, 20)
    f32 = jnp.float32

    def nrm(k, shape, scale):
        return jax.random.normal(k, shape, f32) * scale

    def gain(k, shape):
        return 1.0 + 0.01 * jax.random.normal(k, shape, f32)

    return {
        'x': nrm(ks[0], (BATCH, SEQ, D_MODEL), 1.0),
        'p': nrm(ks[1], (DEPTH, BATCH, SEQ, PLE_DIM), 1.0),
        'g_mix': gain(ks[2], (DEPTH, D_MODEL)),
        'w_in': nrm(ks[3], (DEPTH, D_MODEL, IN_COLS), D_MODEL ** -0.5),
        'rpb': nrm(ks[4], (DEPTH, N_ATTN_HEADS, 2 * WIN_H_MAX - 1, 2 * WIN_W - 1), 0.02),
        'conv_w': nrm(ks[5], (DEPTH, CONV_K, CONV_W), CONV_K ** -0.5),
        'g_attn_out': gain(ks[6], (DEPTH, ATTN_W)),
        'g_conv_out': gain(ks[7], (DEPTH, CONV_W)),
        'w_out': nrm(ks[8], (DEPTH, MIX_W, D_MODEL), MIX_W ** -0.5),
        'g_ffn': gain(ks[9], (DEPTH, D_MODEL)),
        'w_gate': nrm(ks[10], (DEPTH, D_MODEL, D_FF), D_MODEL ** -0.5),
        'w_up': nrm(ks[11], (DEPTH, D_MODEL, D_FF), D_MODEL ** -0.5),
        'w_down': nrm(ks[12], (DEPTH, D_FF, D_MODEL), D_FF ** -0.5),
        'g_ple': gain(ks[13], (DEPTH, D_MODEL)),
        'w_ple_gate': nrm(ks[14], (DEPTH, D_MODEL, D_MODEL), D_MODEL ** -0.5),
        'w_ple_proj': nrm(ks[15], (DEPTH, PLE_DIM, D_MODEL), PLE_DIM ** -0.5),
        'g_final': gain(ks[16], (D_MODEL,)),
    }


def reference(x, p, g_mix, w_in, rpb, conv_w, g_attn_out, g_conv_out, w_out,
              g_ffn, w_gate, w_up, w_down, g_ple, w_ple_gate, w_ple_proj, g_final):
    b, s, _ = x.shape
    h = x
    for i in range(DEPTH):
        hn = rmsnorm(h, g_mix[i])
        z = hn @ w_in[i]
        q, k, v, cb, cc, cu = jnp.split(
            z, [ATTN_W, 2 * ATTN_W, 3 * ATTN_W, 3 * ATTN_W + CONV_W, 3 * ATTN_W + 2 * CONV_W], axis=-1)
        hd = (b, s, N_ATTN_HEADS, HEAD_DIM)
        attn = neighbourhood_attention(q.reshape(hd), k.reshape(hd), v.reshape(hd), rpb[i])
        conv = cb * centred_depthwise_conv(cc * cu, conv_w[i])
        mixed = jnp.concatenate([rmsnorm(attn, g_attn_out[i]), rmsnorm(conv, g_conv_out[i])], axis=-1)
        h = h + mixed @ w_out[i]
        hn = rmsnorm(h, g_ffn[i])
        h = h + (jax.nn.silu(hn @ w_gate[i]) * (hn @ w_up[i])) @ w_down[i]
        hn = rmsnorm(h, g_ple[i])
        h = h + jax.nn.sigmoid(hn @ w_ple_gate[i]) * (p[i] @ w_ple_proj[i])
    return rmsnorm(h, g_final)
```

```python
import functools

import jax
import jax.numpy as jnp
from jax import lax
from jax.experimental import pallas as pl
from jax.experimental.pallas import tpu as pltpu

F32 = jnp.float32
BF16 = jnp.bfloat16

D_MODEL = 1024
GRID_W = 64
ATTN_W = 512
CONV_W = 512
HEAD_DIM = 64
N_HEADS = ATTN_W // HEAD_DIM
WIN_H = 8
WIN_W = 16
CONV_K = 3
D_FF = 2816
PLE_DIM = 256
EPS = 1e-6

HEADS_PER_GROUP = 4
GROUP_W = HEADS_PER_GROUP * HEAD_DIM
N_GROUPS = N_HEADS // HEADS_PER_GROUP
WIN_KEYS = WIN_H * GRID_W
MASK_VALUE = -1e30

TM_PROJ = 512
ROWS_PER_STEP = 8
TM_POST = 512
HALO = 8
VMEM_LIMIT = 60000 * 1024


def _rms(x, g):
    ms = jnp.mean(x * x, axis=-1, keepdims=True)
    return x * lax.rsqrt(ms + EPS) * g


def _proj_in_kernel(h_ref, g_ref, w_ref, qkv_ref, cb_ref, u_ref):
    hn = _rms(h_ref[0], g_ref[...]).astype(BF16)
    qkv_ref[0] = jnp.dot(hn, w_ref[:, : 3 * ATTN_W], preferred_element_type=F32).astype(BF16)
    c0 = 3 * ATTN_W
    cb_ref[0] = jnp.dot(hn, w_ref[:, c0 : c0 + CONV_W], preferred_element_type=F32)
    cc = jnp.dot(hn, w_ref[:, c0 + CONV_W : c0 + 2 * CONV_W], preferred_element_type=F32)
    cu = jnp.dot(hn, w_ref[:, c0 + 2 * CONV_W : c0 + 3 * CONV_W], preferred_element_type=F32)
    u_ref[0] = cc * cu


def _proj_in(h, g_mix, w_in, layer):
    b, s, d = h.shape
    n_in = w_in.shape[-1]
    tm = TM_PROJ
    return pl.pallas_call(
        _proj_in_kernel,
        grid=(b, s // tm),
        in_specs=[
            pl.BlockSpec((1, tm, d), lambda i, j: (i, j, 0)),
            pl.BlockSpec((None, 1, d), lambda i, j: (layer, 0, 0)),
            pl.BlockSpec((None, d, n_in), lambda i, j: (layer, 0, 0)),
        ],
        out_specs=[
            pl.BlockSpec((1, tm, 3 * ATTN_W), lambda i, j: (i, j, 0)),
            pl.BlockSpec((1, tm, CONV_W), lambda i, j: (i, j, 0)),
            pl.BlockSpec((1, tm, CONV_W), lambda i, j: (i, j, 0)),
        ],
        out_shape=[
            jax.ShapeDtypeStruct((b, s, 3 * ATTN_W), BF16),
            jax.ShapeDtypeStruct((b, s, CONV_W), F32),
            jax.ShapeDtypeStruct((b, s, CONV_W), F32),
        ],
        compiler_params=pltpu.CompilerParams(
            dimension_semantics=("parallel", "parallel"), vmem_limit_bytes=VMEM_LIMIT
        ),
        name="proj_in",
    )(h, g_mix, w_in)


def _bias_tables(rpb):
    depth = rpb.shape[0]
    cols = jnp.arange(GRID_W)
    col_start = jnp.clip(cols - WIN_W // 2, 0, GRID_W - WIN_W)
    kc = jnp.arange(GRID_W)
    valid = (kc[None, :] >= col_start[:, None]) & (kc[None, :] < col_start[:, None] + WIN_W)
    off = jnp.clip(kc[None, :] - cols[:, None] + (WIN_W - 1), 0, 2 * WIN_W - 2)
    colpart = rpb[:, :, :, off]
    colpart = jnp.where(valid[None, None, None], colpart, MASK_VALUE)
    dlt = jnp.arange(WIN_H)
    kk = jnp.arange(WIN_H)
    row_off = kk[None, :] - dlt[:, None] + (WIN_H - 1)
    tbl = colpart[:, :, row_off]
    tbl = tbl.transpose(0, 2, 1, 4, 3, 5)
    return tbl.reshape(depth, WIN_H, N_GROUPS, HEADS_PER_GROUP * GRID_W, WIN_KEYS).astype(F32)


def _attention_kernel(q_ref, k_ref, v_ref, bias_ref, g_ref, o_ref):
    rb = pl.program_id(1)
    n_rows = k_ref.shape[1] // GRID_W
    lane_head = lax.broadcasted_iota(jnp.int32, (GRID_W, GROUP_W), 1) // HEAD_DIM

    def row_body(i, carry):
        r = rb * ROWS_PER_STEP + i
        r0 = jnp.clip(r - WIN_H // 2, 0, n_rows - WIN_H)
        dlt = r - r0
        tok = pl.multiple_of(i * GRID_W, GRID_W)
        key0 = pl.multiple_of(r0 * GRID_W, GRID_W)
        q_row = q_ref[0, pl.ds(tok, GRID_W), :]
        groups = []
        for g in range(N_GROUPS):
            lanes = slice(g * GROUP_W, (g + 1) * GROUP_W)
            q4 = q_row[:, lanes] * jnp.asarray(HEAD_DIM ** -0.5, BF16)
            lhs = jnp.concatenate(
                [jnp.where(lane_head == h, q4, jnp.zeros_like(q4)) for h in range(HEADS_PER_GROUP)],
                axis=0,
            )
            k_win = k_ref[0, pl.ds(key0, WIN_KEYS), lanes]
            v_win = v_ref[0, pl.ds(key0, WIN_KEYS), lanes]
            sc = lax.dot_general(lhs, k_win, (((1,), (1,)), ((), ())), preferred_element_type=F32)
            sc = sc + bias_ref[dlt, g]
            m = jnp.max(sc, axis=-1, keepdims=True)
            e = jnp.exp(sc - m)
            denom = jnp.sum(e, axis=-1, keepdims=True)
            o = jnp.dot(e.astype(BF16), v_win, preferred_element_type=F32) / denom
            o4 = jnp.zeros((GRID_W, GROUP_W), F32)
            for h in range(HEADS_PER_GROUP):
                o4 = o4 + jnp.where(lane_head == h, o[h * GRID_W : (h + 1) * GRID_W], 0.0)
            groups.append(o4)
        attn = jnp.concatenate(groups, axis=1)
        o_ref[0, pl.ds(tok, GRID_W), :] = _rms(attn, g_ref[...]).astype(BF16)
        return carry

    lax.fori_loop(0, ROWS_PER_STEP, row_body, 0)


def _attention(qkv, bias_tbl, g_attn_out, layer):
    b, s, _ = qkv.shape
    tq = ROWS_PER_STEP * GRID_W
    return pl.pallas_call(
        _attention_kernel,
        grid=(b, s // tq),
        in_specs=[
            pl.BlockSpec((1, tq, ATTN_W), lambda i, j: (i, j, 0)),
            pl.BlockSpec((1, s, ATTN_W), lambda i, j: (i, 0, 1)),
            pl.BlockSpec((1, s, ATTN_W), lambda i, j: (i, 0, 2)),
            pl.BlockSpec(
                (None, WIN_H, N_GROUPS, HEADS_PER_GROUP * GRID_W, WIN_KEYS),
                lambda i, j: (layer, 0, 0, 0, 0),
            ),
            pl.BlockSpec((None, 1, ATTN_W), lambda i, j: (layer, 0, 0)),
        ],
        out_specs=pl.BlockSpec((1, tq, ATTN_W), lambda i, j: (i, j, 0)),
        out_shape=jax.ShapeDtypeStruct((b, s, ATTN_W), BF16),
        compiler_params=pltpu.CompilerParams(
            dimension_semantics=("parallel", "arbitrary"), vmem_limit_bytes=VMEM_LIMIT
        ),
        name="attention",
    )(qkv, qkv, qkv, bias_tbl, g_attn_out)


def _post_kernel(
    h_ref, attn_ref, cb_ref, u_ref, up_ref, un_ref, p_ref,
    cw_ref, gconv_ref, wout_ref, gffn_ref, wgate_ref, wup_ref, wdown_ref,
    gple_ref, wpg_ref, wpp_ref, gfin_ref, o_ref, *, last,
):
    j = pl.program_id(1)
    nj = pl.num_programs(1)
    tm = h_ref.shape[1]

    u = u_ref[0]
    row = lax.broadcasted_iota(jnp.int32, u.shape, 0)
    prev_row = jnp.where(j == 0, 0.0, up_ref[0, HALO - 1 : HALO, :])
    next_row = jnp.where(j == nj - 1, 0.0, un_ref[0, 0:1, :])
    u_prev = jnp.where(row == 0, prev_row, pltpu.roll(u, 1, 0))
    u_next = jnp.where(row == tm - 1, next_row, pltpu.roll(u, tm - 1, 0))
    conv = cb_ref[0] * (cw_ref[0:1, :] * u_prev + cw_ref[1:2, :] * u + cw_ref[2:3, :] * u_next)
    conv_n = _rms(conv, gconv_ref[...]).astype(BF16)

    mixed = jnp.concatenate([attn_ref[0], conv_n], axis=1)
    h1 = h_ref[0] + jnp.dot(mixed, wout_ref[...], preferred_element_type=F32)

    hn = _rms(h1, gffn_ref[...]).astype(BF16)
    gate = jnp.dot(hn, wgate_ref[...], preferred_element_type=F32)
    up = jnp.dot(hn, wup_ref[...], preferred_element_type=F32)
    act = (gate * jax.nn.sigmoid(gate) * up).astype(BF16)
    h2 = h1 + jnp.dot(act, wdown_ref[...], preferred_element_type=F32)

    hn = _rms(h2, gple_ref[...]).astype(BF16)
    gate = jax.nn.sigmoid(jnp.dot(hn, wpg_ref[...], preferred_element_type=F32))
    proj = jnp.dot(p_ref[0].astype(BF16), wpp_ref[...], preferred_element_type=F32)
    h3 = h2 + gate * proj
    if last:
        h3 = _rms(h3, gfin_ref[...])
    o_ref[0] = h3


def _post(h, attn, cb, u, p, conv_w, g_conv_out, w_out, g_ffn, w_gate, w_up, w_down,
          g_ple, w_ple_gate, w_ple_proj, g_final, layer, last):
    b, s, d = h.shape
    tm = TM_POST
    hb = tm // HALO
    n_halo = s // HALO

    def tile(width):
        return pl.BlockSpec((1, tm, width), lambda i, j: (i, j, 0))

    def resident(shape):
        zeros = (0,) * len(shape)
        return pl.BlockSpec((None,) + shape, lambda i, j: (layer,) + zeros,
                            pipeline_mode=pl.Buffered(1))

    in_specs = [
        tile(d),
        tile(ATTN_W),
        tile(CONV_W),
        tile(CONV_W),
        pl.BlockSpec((1, HALO, CONV_W), lambda i, j: (i, jnp.maximum(j * hb - 1, 0), 0)),
        pl.BlockSpec((1, HALO, CONV_W), lambda i, j: (i, jnp.minimum((j + 1) * hb, n_halo - 1), 0)),
        pl.BlockSpec((None, 1, tm, PLE_DIM), lambda i, j: (layer, i, j, 0)),
        resident((CONV_K, CONV_W)),
        resident((1, CONV_W)),
        resident((d, d)),
        resident((1, d)),
        resident((d, D_FF)),
        resident((d, D_FF)),
        resident((D_FF, d)),
        resident((1, d)),
        resident((d, d)),
        resident((PLE_DIM, d)),
        pl.BlockSpec((1, d), lambda i, j: (0, 0)),
    ]
    return pl.pallas_call(
        functools.partial(_post_kernel, last=last),
        grid=(b, s // tm),
        in_specs=in_specs,
        out_specs=tile(d),
        out_shape=jax.ShapeDtypeStruct((b, s, d), F32),
        compiler_params=pltpu.CompilerParams(
            dimension_semantics=("parallel", "parallel"), vmem_limit_bytes=VMEM_LIMIT
        ),
        name="post",
    )(h, attn, cb, u, u, u, p, conv_w, g_conv_out, w_out, g_ffn, w_gate, w_up, w_down,
      g_ple, w_ple_gate, w_ple_proj, g_final)


def kernel(x, p, g_mix, w_in, rpb, conv_w, g_attn_out, g_conv_out, w_out, g_ffn, w_gate, w_up,
           w_down, g_ple, w_ple_gate, w_ple_proj, g_final):
    depth = w_in.shape[0]
    row = lambda g: g.reshape(g.shape[0], 1, g.shape[1])
    g_mix, g_attn_out, g_conv_out, g_ffn, g_ple = map(row, (g_mix, g_attn_out, g_conv_out, g_ffn, g_ple))
    g_final = g_final.reshape(1, -1)
    w_in, w_out, w_gate, w_up, w_down, w_ple_gate, w_ple_proj = (
        w.astype(BF16) for w in (w_in, w_out, w_gate, w_up, w_down, w_ple_gate, w_ple_proj)
    )
    bias_tbl = _bias_tables(rpb)

    h = x
    for layer in range(depth):
        qkv, cb, u = _proj_in(h, g_mix, w_in, layer)
        attn = _attention(qkv, bias_tbl, g_attn_out, layer)
        h = _post(h, attn, cb, u, p, conv_w, g_conv_out, w_out, g_ffn, w_gate, w_up, w_down,
                  g_ple, w_ple_gate, w_ple_proj, g_final, layer, layer == depth - 1)
    return h
```

```python
import functools

import jax
import jax.numpy as jnp
from jax import lax
from jax.experimental import pallas as pl
from jax.experimental.pallas import tpu as pltpu

F32 = jnp.float32
BF16 = jnp.bfloat16

D_MODEL = 1024
GRID_W = 64
ATTN_W = 512
CONV_W = 512
HEAD_DIM = 64
N_HEADS = ATTN_W // HEAD_DIM
WIN_H = 8
WIN_W = 16
CONV_K = 3
D_FF = 2816
PLE_DIM = 256
EPS = 1e-6

HEADS_PER_GROUP = 4
GROUP_W = HEADS_PER_GROUP * HEAD_DIM
N_GROUPS = N_HEADS // HEADS_PER_GROUP
WIN_KEYS = WIN_H * GRID_W
MASK_VALUE = -1e30
LOG2E = 1.4426950408889634
Q_SCALE = HEAD_DIM ** -0.5 * LOG2E

TM_PROJ = 512
PAIR_UNROLL = 2
TM_POST = 512
HALO = 8
LANES = 128
VMEM_LIMIT = 60000 * 1024


def _rms(x, g):
    ms = jnp.mean(x * x, axis=-1, keepdims=True)
    return x * lax.rsqrt(ms + EPS) * g


def _proj_in_kernel(h_ref, g_ref, w_ref, qkv_ref, cb_ref, u_ref):
    hn = _rms(h_ref[0], g_ref[...]).astype(BF16)
    q = jnp.dot(hn, w_ref[:, :ATTN_W], preferred_element_type=F32) * Q_SCALE
    qkv_ref[0, :, :ATTN_W] = q.astype(BF16)
    kv = jnp.dot(hn, w_ref[:, ATTN_W : 3 * ATTN_W], preferred_element_type=F32)
    qkv_ref[0, :, ATTN_W:] = kv.astype(BF16)
    c0 = 3 * ATTN_W
    cb_ref[0] = jnp.dot(hn, w_ref[:, c0 : c0 + CONV_W], preferred_element_type=F32)
    cc = jnp.dot(hn, w_ref[:, c0 + CONV_W : c0 + 2 * CONV_W], preferred_element_type=F32)
    cu = jnp.dot(hn, w_ref[:, c0 + 2 * CONV_W : c0 + 3 * CONV_W], preferred_element_type=F32)
    u_ref[0] = cc * cu


def _proj_in(h, g_mix, w_in, layer):
    b, s, d = h.shape
    n_in = w_in.shape[-1]
    tm = TM_PROJ
    return pl.pallas_call(
        _proj_in_kernel,
        grid=(b, s // tm),
        in_specs=[
            pl.BlockSpec((1, tm, d), lambda i, j: (i, j, 0)),
            pl.BlockSpec((None, 1, d), lambda i, j: (layer, 0, 0)),
            pl.BlockSpec((None, d, n_in), lambda i, j: (layer, 0, 0)),
        ],
        out_specs=[
            pl.BlockSpec((1, tm, 3 * ATTN_W), lambda i, j: (i, j, 0)),
            pl.BlockSpec((1, tm, CONV_W), lambda i, j: (i, j, 0)),
            pl.BlockSpec((1, tm, CONV_W), lambda i, j: (i, j, 0)),
        ],
        out_shape=[
            jax.ShapeDtypeStruct((b, s, 3 * ATTN_W), BF16),
            jax.ShapeDtypeStruct((b, s, CONV_W), F32),
            jax.ShapeDtypeStruct((b, s, CONV_W), F32),
        ],
        compiler_params=pltpu.CompilerParams(
            dimension_semantics=("parallel", "parallel"), vmem_limit_bytes=VMEM_LIMIT
        ),
        name="proj_in",
    )(h, g_mix, w_in)


def _bias_tables(rpb):
    depth = rpb.shape[0]
    cols = jnp.arange(GRID_W)
    col_start = jnp.clip(cols - WIN_W // 2, 0, GRID_W - WIN_W)
    kc = jnp.arange(GRID_W)
    valid = (kc[None, :] >= col_start[:, None]) & (kc[None, :] < col_start[:, None] + WIN_W)
    off = jnp.clip(kc[None, :] - cols[:, None] + (WIN_W - 1), 0, 2 * WIN_W - 2)
    colpart = rpb[:, :, :, off] * LOG2E
    colpart = jnp.where(valid[None, None, None], colpart, MASK_VALUE)
    dlt = jnp.arange(WIN_H)
    kk = jnp.arange(WIN_H)
    row_off = kk[None, :] - dlt[:, None] + (WIN_H - 1)
    tbl = colpart[:, :, row_off]
    tbl = tbl.transpose(0, 2, 1, 4, 3, 5)
    return tbl.reshape(depth, WIN_H, N_GROUPS, HEADS_PER_GROUP * GRID_W, WIN_KEYS).astype(F32)


def _attention_kernel(q_ref, k_ref, v_ref, bias_ref, g_ref, o_ref, s_scr, p_scr, l_scr):
    n_rows = k_ref.shape[1] // GRID_W
    rows_hc = HEADS_PER_GROUP * GRID_W
    lane_head = lax.broadcasted_iota(jnp.int32, (GRID_W, GROUP_W), 1) // HEAD_DIM

    def window(r):
        r0 = jnp.clip(r - WIN_H // 2, 0, n_rows - WIN_H)
        return r - r0, pl.multiple_of(r0 * GRID_W, GRID_W)

    def logits(r, slot):
        dlt, key0 = window(r)
        q_row = q_ref[0, pl.ds(pl.multiple_of(r * GRID_W, GRID_W), GRID_W), :]
        for g in range(N_GROUPS):
            lanes = slice(g * GROUP_W, (g + 1) * GROUP_W)
            q4 = q_row[:, lanes]
            lhs = jnp.concatenate(
                [jnp.where(lane_head == h, q4, jnp.zeros_like(q4)) for h in range(HEADS_PER_GROUP)],
                axis=0,
            )
            k_win = k_ref[0, pl.ds(key0, WIN_KEYS), lanes]
            sc = lax.dot_general(lhs, k_win, (((1,), (1,)), ((), ())), preferred_element_type=F32)
            s_scr[slot, g] = sc + bias_ref[dlt, g]

    def softmax(slot):
        for g in range(N_GROUPS):
            sc = s_scr[slot, g]
            e = jnp.exp2(sc - jnp.max(sc, axis=-1, keepdims=True))
            l_scr[slot, g] = jnp.broadcast_to(jnp.sum(e, axis=-1, keepdims=True), (rows_hc, LANES))
            p_scr[slot, g] = e.astype(BF16)

    def output(r, slot):
        _, key0 = window(r)
        groups = []
        for g in range(N_GROUPS):
            lanes = slice(g * GROUP_W, (g + 1) * GROUP_W)
            v_win = v_ref[0, pl.ds(key0, WIN_KEYS), lanes]
            inv_l = 1.0 / l_scr[slot, g]
            o = jnp.dot(p_scr[slot, g], v_win, preferred_element_type=F32)
            o = o * jnp.concatenate([inv_l] * (GROUP_W // LANES), axis=1)
            o4 = jnp.zeros((GRID_W, GROUP_W), F32)
            for h in range(HEADS_PER_GROUP):
                o4 = o4 + jnp.where(lane_head == h, o[h * GRID_W : (h + 1) * GRID_W], 0.0)
            groups.append(o4)
        attn = jnp.concatenate(groups, axis=1)
        tok = pl.multiple_of(r * GRID_W, GRID_W)
        o_ref[0, pl.ds(tok, GRID_W), :] = _rms(attn, g_ref[...]).astype(BF16)

    def pair(j, carry):
        i = 2 * j
        logits(i, 0)
        softmax(1)
        output(i - 2, 0)
        logits(i + 1, 1)
        softmax(0)
        output(i - 1, 1)
        return carry

    logits(0, 0)
    logits(1, 1)
    softmax(0)
    lax.fori_loop(1, n_rows // 2, pair, 0, unroll=PAIR_UNROLL)
    softmax(1)
    output(n_rows - 2, 0)
    output(n_rows - 1, 1)


def _attention(qkv, bias_tbl, g_attn_out, layer):
    b, s, _ = qkv.shape
    rows_hc = HEADS_PER_GROUP * GRID_W
    image = lambda col: pl.BlockSpec((1, s, ATTN_W), lambda i: (i, 0, col))
    return pl.pallas_call(
        _attention_kernel,
        grid=(b,),
        in_specs=[
            image(0),
            image(1),
            image(2),
            pl.BlockSpec(
                (None, WIN_H, N_GROUPS, rows_hc, WIN_KEYS),
                lambda i: (layer, 0, 0, 0, 0),
                pipeline_mode=pl.Buffered(1),
            ),
            pl.BlockSpec((None, 1, ATTN_W), lambda i: (layer, 0, 0)),
        ],
        out_specs=image(0),
        out_shape=jax.ShapeDtypeStruct((b, s, ATTN_W), BF16),
        scratch_shapes=[
            pltpu.VMEM((2, N_GROUPS, rows_hc, WIN_KEYS), F32),
            pltpu.VMEM((2, N_GROUPS, rows_hc, WIN_KEYS), BF16),
            pltpu.VMEM((2, N_GROUPS, rows_hc, LANES), F32),
        ],
        compiler_params=pltpu.CompilerParams(
            dimension_semantics=("parallel",), vmem_limit_bytes=VMEM_LIMIT
        ),
        name="attention",
    )(qkv, qkv, qkv, bias_tbl, g_attn_out)


def _post_kernel(
    h_ref, attn_ref, cb_ref, u_ref, up_ref, un_ref, p_ref,
    cw_ref, gconv_ref, wout_ref, gffn_ref, wgate_ref, wup_ref, wdown_ref,
    gple_ref, wpg_ref, wpp_ref, gfin_ref, o_ref, *, last,
):
    j = pl.program_id(1)
    nj = pl.num_programs(1)
    tm = h_ref.shape[1]

    u = u_ref[0]
    row = lax.broadcasted_iota(jnp.int32, u.shape, 0)
    prev_row = jnp.where(j == 0, 0.0, up_ref[0, HALO - 1 : HALO, :])
    next_row = jnp.where(j == nj - 1, 0.0, un_ref[0, 0:1, :])
    u_prev = jnp.where(row == 0, prev_row, pltpu.roll(u, 1, 0))
    u_next = jnp.where(row == tm - 1, next_row, pltpu.roll(u, tm - 1, 0))
    conv = cb_ref[0] * (cw_ref[0:1, :] * u_prev + cw_ref[1:2, :] * u + cw_ref[2:3, :] * u_next)
    conv_n = _rms(conv, gconv_ref[...]).astype(BF16)

    mixed = jnp.concatenate([attn_ref[0], conv_n], axis=1)
    h1 = h_ref[0] + jnp.dot(mixed, wout_ref[...], preferred_element_type=F32)

    hn = _rms(h1, gffn_ref[...]).astype(BF16)
    gate = jnp.dot(hn, wgate_ref[...], preferred_element_type=F32)
    up = jnp.dot(hn, wup_ref[...], preferred_element_type=F32)
    act = (gate * jax.nn.sigmoid(gate) * up).astype(BF16)
    h2 = h1 + jnp.dot(act, wdown_ref[...], preferred_element_type=F32)

    hn = _rms(h2, gple_ref[...]).astype(BF16)
    gate = jax.nn.sigmoid(jnp.dot(hn, wpg_ref[...], preferred_element_type=F32))
    proj = jnp.dot(p_ref[0].astype(BF16), wpp_ref[...], preferred_element_type=F32)
    h3 = h2 + gate * proj
    if last:
        h3 = _rms(h3, gfin_ref[...])
    o_ref[0] = h3


def _post(h, attn, cb, u, p, conv_w, g_conv_out, w_out, g_ffn, w_gate, w_up, w_down,
          g_ple, w_ple_gate, w_ple_proj, g_final, layer, last):
    b, s, d = h.shape
    tm = TM_POST
    hb = tm // HALO
    n_halo = s // HALO

    def tile(width):
        return pl.BlockSpec((1, tm, width), lambda i, j: (i, j, 0))

    def resident(shape):
        zeros = (0,) * len(shape)
        return pl.BlockSpec((None,) + shape, lambda i, j: (layer,) + zeros,
                            pipeline_mode=pl.Buffered(1))

    in_specs = [
        tile(d),
        tile(ATTN_W),
        tile(CONV_W),
        tile(CONV_W),
        pl.BlockSpec((1, HALO, CONV_W), lambda i, j: (i, jnp.maximum(j * hb - 1, 0), 0)),
        pl.BlockSpec((1, HALO, CONV_W), lambda i, j: (i, jnp.minimum((j + 1) * hb, n_halo - 1), 0)),
        pl.BlockSpec((None, 1, tm, PLE_DIM), lambda i, j: (layer, i, j, 0)),
        resident((CONV_K, CONV_W)),
        resident((1, CONV_W)),
        resident((d, d)),
        resident((1, d)),
        resident((d, D_FF)),
        resident((d, D_FF)),
        resident((D_FF, d)),
        resident((1, d)),
        resident((d, d)),
        resident((PLE_DIM, d)),
        pl.BlockSpec((1, d), lambda i, j: (0, 0)),
    ]
    return pl.pallas_call(
        functools.partial(_post_kernel, last=last),
        grid=(b, s // tm),
        in_specs=in_specs,
        out_specs=tile(d),
        out_shape=jax.ShapeDtypeStruct((b, s, d), F32),
        compiler_params=pltpu.CompilerParams(
            dimension_semantics=("parallel", "parallel"), vmem_limit_bytes=VMEM_LIMIT
        ),
        name="post",
    )(h, attn, cb, u, u, u, p, conv_w, g_conv_out, w_out, g_ffn, w_gate, w_up, w_down,
      g_ple, w_ple_gate, w_ple_proj, g_final)


def kernel(x, p, g_mix, w_in, rpb, conv_w, g_attn_out, g_conv_out, w_out, g_ffn, w_gate, w_up,
           w_down, g_ple, w_ple_gate, w_ple_proj, g_final):
    depth = w_in.shape[0]
    row = lambda g: g.reshape(g.shape[0], 1, g.shape[1])
    g_mix, g_attn_out, g_conv_out, g_ffn, g_ple = map(row, (g_mix, g_attn_out, g_conv_out, g_ffn, g_ple))
    g_final = g_final.reshape(1, -1)
    w_in, w_out, w_gate, w_up, w_down, w_ple_gate, w_ple_proj = (
        w.astype(BF16) for w in (w_in, w_out, w_gate, w_up, w_down, w_ple_gate, w_ple_proj)
    )
    bias_tbl = _bias_tables(rpb)

    h = x
    for layer in range(depth):
        qkv, cb, u = _proj_in(h, g_mix, w_in, layer)
        attn = _attention(qkv, bias_tbl, g_attn_out, layer)
        h = _post(h, attn, cb, u, p, conv_w, g_conv_out, w_out, g_ffn, w_gate, w_up, w_down,
                  g_ple, w_ple_gate, w_ple_proj, g_final, layer, layer == depth - 1)
    return h
```

```python
import functools

import jax
import jax.numpy as jnp
from jax import lax
from jax.experimental import pallas as pl
from jax.experimental.pallas import tpu as pltpu

F32 = jnp.float32
BF16 = jnp.bfloat16

D_MODEL = 1024
GRID_W = 64
ATTN_W = 512
CONV_W = 512
HEAD_DIM = 64
N_HEADS = ATTN_W // HEAD_DIM
WIN_H = 8
WIN_W = 16
CONV_K = 3
D_FF = 2816
PLE_DIM = 256
EPS = 1e-6

HEADS_PER_GROUP = 4
GROUP_W = HEADS_PER_GROUP * HEAD_DIM
N_GROUPS = N_HEADS // HEADS_PER_GROUP
WIN_KEYS = WIN_H * GRID_W
SLAB = WIN_W
N_SLABS = GRID_W // SLAB
SLAB_ROWS = HEADS_PER_GROUP * SLAB
HALF_KEYS = WIN_KEYS // 2
HALF_ROWS = (N_SLABS - 1) * SLAB_ROWS
MASK_VALUE = -1e30
LOG2E = 1.4426950408889634
Q_SCALE = HEAD_DIM ** -0.5 * LOG2E

TM_PROJ = 512
PAIR_UNROLL = 2
TM_POST = 512
HALO = 8
LANES = 128
VMEM_LIMIT = 60000 * 1024


def _rms(x, g):
    ms = jnp.mean(x * x, axis=-1, keepdims=True)
    return x * lax.rsqrt(ms + EPS) * g


def _proj_in_kernel(h_ref, g_ref, w_ref, qkv_ref, cb_ref, u_ref):
    hn = _rms(h_ref[0], g_ref[...]).astype(BF16)
    q = jnp.dot(hn, w_ref[:, :ATTN_W], preferred_element_type=F32) * Q_SCALE
    qkv_ref[0, :, :ATTN_W] = q.astype(BF16)
    kv = jnp.dot(hn, w_ref[:, ATTN_W : 3 * ATTN_W], preferred_element_type=F32)
    qkv_ref[0, :, ATTN_W:] = kv.astype(BF16)
    c0 = 3 * ATTN_W
    cb_ref[0] = jnp.dot(hn, w_ref[:, c0 : c0 + CONV_W], preferred_element_type=F32)
    cc = jnp.dot(hn, w_ref[:, c0 + CONV_W : c0 + 2 * CONV_W], preferred_element_type=F32)
    cu = jnp.dot(hn, w_ref[:, c0 + 2 * CONV_W : c0 + 3 * CONV_W], preferred_element_type=F32)
    u_ref[0] = cc * cu


def _proj_in(h, g_mix, w_in, layer):
    b, s, d = h.shape
    n_in = w_in.shape[-1]
    tm = TM_PROJ
    return pl.pallas_call(
        _proj_in_kernel,
        grid=(b, s // tm),
        in_specs=[
            pl.BlockSpec((1, tm, d), lambda i, j: (i, j, 0)),
            pl.BlockSpec((None, 1, d), lambda i, j: (layer, 0, 0)),
            pl.BlockSpec((None, d, n_in), lambda i, j: (layer, 0, 0)),
        ],
        out_specs=[
            pl.BlockSpec((1, tm, 3 * ATTN_W), lambda i, j: (i, j, 0)),
            pl.BlockSpec((1, tm, CONV_W), lambda i, j: (i, j, 0)),
            pl.BlockSpec((1, tm, CONV_W), lambda i, j: (i, j, 0)),
        ],
        out_shape=[
            jax.ShapeDtypeStruct((b, s, 3 * ATTN_W), BF16),
            jax.ShapeDtypeStruct((b, s, CONV_W), F32),
            jax.ShapeDtypeStruct((b, s, CONV_W), F32),
        ],
        compiler_params=pltpu.CompilerParams(
            dimension_semantics=("parallel", "parallel"), vmem_limit_bytes=VMEM_LIMIT
        ),
        name="proj_in",
    )(h, g_mix, w_in)


def _bias_tables(rpb):
    depth = rpb.shape[0]
    cols = jnp.arange(GRID_W)
    col_start = jnp.clip(cols - WIN_W // 2, 0, GRID_W - WIN_W)
    kc = jnp.arange(GRID_W)
    valid = (kc[None, :] >= col_start[:, None]) & (kc[None, :] < col_start[:, None] + WIN_W)
    off = jnp.clip(kc[None, :] - cols[:, None] + (WIN_W - 1), 0, 2 * WIN_W - 2)
    colpart = rpb[:, :, :, off] * LOG2E
    colpart = jnp.where(valid[None, None, None], colpart, MASK_VALUE)
    dlt = jnp.arange(WIN_H)
    kk = jnp.arange(WIN_H)
    row_off = kk[None, :] - dlt[:, None] + (WIN_H - 1)
    tbl = colpart[:, :, row_off]
    tbl = tbl.reshape(depth, N_GROUPS, HEADS_PER_GROUP, WIN_H, WIN_H, N_SLABS, SLAB, N_SLABS, SLAB)
    tbl = tbl.transpose(0, 3, 1, 5, 2, 6, 7, 4, 8)
    tbl = tbl.reshape(depth, WIN_H, N_GROUPS, N_SLABS * SLAB_ROWS, WIN_KEYS).astype(F32)
    return tbl[..., :HALF_ROWS, :HALF_KEYS], tbl[..., SLAB_ROWS:, HALF_KEYS:]


_ACTIVE_TILES = (
    ((0, 0), (0, 1)),
    ((0, 0), (0, 1), (1, 0)),
    ((0, 1), (1, 0), (1, 1)),
    ((1, 0), (1, 1)),
)


def _attention_kernel(q_ref, k_ref, v_ref, bias_a_ref, bias_b_ref, g_ref, o_ref,
                      sa_scr, sb_scr, pa_scr, pb_scr, l_scr):
    n_rows = k_ref.shape[1] // GRID_W
    lane_head = lax.broadcasted_iota(jnp.int32, (SLAB, GROUP_W), 1) // HEAD_DIM
    s_scr = (sa_scr, sb_scr)
    p_scr = (pa_scr, pb_scr)
    bias_ref = (bias_a_ref, bias_b_ref)

    pa_scr[:, :, 2 * SLAB_ROWS :, :LANES] = jnp.zeros((2, N_GROUPS, SLAB_ROWS, LANES), BF16)
    pb_scr[:, :, :SLAB_ROWS, LANES:] = jnp.zeros((2, N_GROUPS, SLAB_ROWS, LANES), BF16)

    def window(r):
        r0 = jnp.clip(r - WIN_H // 2, 0, n_rows - WIN_H)
        return r - r0, r0 * GRID_W

    def half_window(ref, key0, lanes, half):
        chunks = []
        for kb in (2 * half, 2 * half + 1):
            for kk in range(WIN_H):
                start = pl.multiple_of(key0 + (kk * GRID_W + kb * SLAB), SLAB)
                chunks.append(ref[0, pl.ds(start, SLAB), lanes])
        return jnp.concatenate(chunks, axis=0)

    def logits(r, slot):
        dlt, key0 = window(r)
        q_row = q_ref[0, pl.ds(pl.multiple_of(r * GRID_W, GRID_W), GRID_W), :]
        for g in range(N_GROUPS):
            lanes = slice(g * GROUP_W, (g + 1) * GROUP_W)
            q4 = q_row[:, lanes]
            lhs = jnp.concatenate(
                [jnp.where(lane_head == h, q4[a * SLAB : (a + 1) * SLAB], jnp.zeros((SLAB, GROUP_W), BF16))
                 for a in range(N_SLABS) for h in range(HEADS_PER_GROUP)],
                axis=0,
            )
            for half in range(2):
                rows = slice(half * SLAB_ROWS, half * SLAB_ROWS + HALF_ROWS)
                k_half = half_window(k_ref, key0, lanes, half)
                sc = lax.dot_general(lhs[rows], k_half, (((1,), (1,)), ((), ())),
                                     preferred_element_type=F32)
                s_scr[half][slot, g] = sc + bias_ref[half][dlt, g]

    def softmax(slot):
        for g in range(N_GROUPS):
            for a in range(N_SLABS):
                tiles_at = [
                    (half, slice((a - half) * SLAB_ROWS, (a - half + 1) * SLAB_ROWS),
                     slice(blk * LANES, (blk + 1) * LANES))
                    for half, blk in _ACTIVE_TILES[a]
                ]
                tiles = [s_scr[half][slot, g, rows, cols] for half, rows, cols in tiles_at]
                m = jnp.max(functools.reduce(jnp.maximum, tiles), axis=-1, keepdims=True)
                es = [jnp.exp2(t - m) for t in tiles]
                l = jnp.sum(functools.reduce(jnp.add, es), axis=-1, keepdims=True)
                l_scr[slot, g, a * SLAB_ROWS : (a + 1) * SLAB_ROWS, :] = jnp.broadcast_to(l, (SLAB_ROWS, LANES))
                for e, (half, rows, cols) in zip(es, tiles_at):
                    p_scr[half][slot, g, rows, cols] = e.astype(BF16)

    def output(r, slot):
        _, key0 = window(r)
        groups = []
        for g in range(N_GROUPS):
            lanes = slice(g * GROUP_W, (g + 1) * GROUP_W)
            oa = jnp.dot(pa_scr[slot, g], half_window(v_ref, key0, lanes, 0), preferred_element_type=F32)
            ob = jnp.dot(pb_scr[slot, g], half_window(v_ref, key0, lanes, 1), preferred_element_type=F32)
            inv_l = 1.0 / l_scr[slot, g]
            slabs = []
            for a in range(N_SLABS):
                parts = []
                if a < N_SLABS - 1:
                    parts.append(oa[a * SLAB_ROWS : (a + 1) * SLAB_ROWS])
                if a > 0:
                    parts.append(ob[(a - 1) * SLAB_ROWS : a * SLAB_ROWS])
                inv = inv_l[a * SLAB_ROWS : (a + 1) * SLAB_ROWS]
                o = functools.reduce(jnp.add, parts) * jnp.concatenate([inv] * (GROUP_W // LANES), axis=1)
                o16 = jnp.zeros((SLAB, GROUP_W), F32)
                for h in range(HEADS_PER_GROUP):
                    o16 = o16 + jnp.where(lane_head == h, o[h * SLAB : (h + 1) * SLAB], 0.0)
                slabs.append(o16)
            groups.append(jnp.concatenate(slabs, axis=0))
        attn = jnp.concatenate(groups, axis=1)
        tok = pl.multiple_of(r * GRID_W, GRID_W)
        o_ref[0, pl.ds(tok, GRID_W), :] = _rms(attn, g_ref[...]).astype(BF16)

    def pair(j, carry):
        i = 2 * j
        logits(i, 0)
        softmax(1)
        output(i - 2, 0)
        logits(i + 1, 1)
        softmax(0)
        output(i - 1, 1)
        return carry

    logits(0, 0)
    logits(1, 1)
    softmax(0)
    lax.fori_loop(1, n_rows // 2, pair, 0, unroll=PAIR_UNROLL)
    softmax(1)
    output(n_rows - 2, 0)
    output(n_rows - 1, 1)


def _attention(qkv, bias_a, bias_b, g_attn_out, layer):
    b, s, _ = qkv.shape
    image = lambda col: pl.BlockSpec((1, s, ATTN_W), lambda i: (i, 0, col))
    table = pl.BlockSpec(
        (None, WIN_H, N_GROUPS, HALF_ROWS, HALF_KEYS),
        lambda i: (layer, 0, 0, 0, 0),
        pipeline_mode=pl.Buffered(1),
    )
    half_scores = lambda dtype: pltpu.VMEM((2, N_GROUPS, HALF_ROWS, HALF_KEYS), dtype)
    return pl.pallas_call(
        _attention_kernel,
        grid=(b,),
        in_specs=[
            image(0),
            image(1),
            image(2),
            table,
            table,
            pl.BlockSpec((None, 1, ATTN_W), lambda i: (layer, 0, 0)),
        ],
        out_specs=image(0),
        out_shape=jax.ShapeDtypeStruct((b, s, ATTN_W), BF16),
        scratch_shapes=[
            half_scores(F32),
            half_scores(F32),
            half_scores(BF16),
            half_scores(BF16),
            pltpu.VMEM((2, N_GROUPS, N_SLABS * SLAB_ROWS, LANES), F32),
        ],
        compiler_params=pltpu.CompilerParams(
            dimension_semantics=("parallel",), vmem_limit_bytes=VMEM_LIMIT
        ),
        name="attention",
    )(qkv, qkv, qkv, bias_a, bias_b, g_attn_out)


def _post_kernel(
    h_ref, attn_ref, cb_ref, u_ref, up_ref, un_ref, p_ref,
    cw_ref, gconv_ref, wout_ref, gffn_ref, wgate_ref, wup_ref, wdown_ref,
    gple_ref, wpg_ref, wpp_ref, gfin_ref, o_ref, *, last,
):
    j = pl.program_id(1)
    nj = pl.num_programs(1)
    tm = h_ref.shape[1]

    u = u_ref[0]
    row = lax.broadcasted_iota(jnp.int32, u.shape, 0)
    prev_row = jnp.where(j == 0, 0.0, up_ref[0, HALO - 1 : HALO, :])
    next_row = jnp.where(j == nj - 1, 0.0, un_ref[0, 0:1, :])
    u_prev = jnp.where(row == 0, prev_row, pltpu.roll(u, 1, 0))
    u_next = jnp.where(row == tm - 1, next_row, pltpu.roll(u, tm - 1, 0))
    conv = cb_ref[0] * (cw_ref[0:1, :] * u_prev + cw_ref[1:2, :] * u + cw_ref[2:3, :] * u_next)
    conv_n = _rms(conv, gconv_ref[...]).astype(BF16)

    mixed = jnp.concatenate([attn_ref[0], conv_n], axis=1)
    h1 = h_ref[0] + jnp.dot(mixed, wout_ref[...], preferred_element_type=F32)

    hn = _rms(h1, gffn_ref[...]).astype(BF16)
    gate = jnp.dot(hn, wgate_ref[...], preferred_element_type=F32)
    up = jnp.dot(hn, wup_ref[...], preferred_element_type=F32)
    act = (gate * jax.nn.sigmoid(gate) * up).astype(BF16)
    h2 = h1 + jnp.dot(act, wdown_ref[...], preferred_element_type=F32)

    hn = _rms(h2, gple_ref[...]).astype(BF16)
    gate = jax.nn.sigmoid(jnp.dot(hn, wpg_ref[...], preferred_element_type=F32))
    proj = jnp.dot(p_ref[0].astype(BF16), wpp_ref[...], preferred_element_type=F32)
    h3 = h2 + gate * proj
    if last:
        h3 = _rms(h3, gfin_ref[...])
    o_ref[0] = h3


def _post(h, attn, cb, u, p, conv_w, g_conv_out, w_out, g_ffn, w_gate, w_up, w_down,
          g_ple, w_ple_gate, w_ple_proj, g_final, layer, last):
    b, s, d = h.shape
    tm = TM_POST
    hb = tm // HALO
    n_halo = s // HALO

    def tile(width):
        return pl.BlockSpec((1, tm, width), lambda i, j: (i, j, 0))

    def resident(shape):
        zeros = (0,) * len(shape)
        return pl.BlockSpec((None,) + shape, lambda i, j: (layer,) + zeros,
                            pipeline_mode=pl.Buffered(1))

    in_specs = [
        tile(d),
        tile(ATTN_W),
        tile(CONV_W),
        tile(CONV_W),
        pl.BlockSpec((1, HALO, CONV_W), lambda i, j: (i, jnp.maximum(j * hb - 1, 0), 0)),
        pl.BlockSpec((1, HALO, CONV_W), lambda i, j: (i, jnp.minimum((j + 1) * hb, n_halo - 1), 0)),
        pl.BlockSpec((None, 1, tm, PLE_DIM), lambda i, j: (layer, i, j, 0)),
        resident((CONV_K, CONV_W)),
        resident((1, CONV_W)),
        resident((d, d)),
        resident((1, d)),
        resident((d, D_FF)),
        resident((d, D_FF)),
        resident((D_FF, d)),
        resident((1, d)),
        resident((d, d)),
        resident((PLE_DIM, d)),
        pl.BlockSpec((1, d), lambda i, j: (0, 0)),
    ]
    return pl.pallas_call(
        functools.partial(_post_kernel, last=last),
        grid=(b, s // tm),
        in_specs=in_specs,
        out_specs=tile(d),
        out_shape=jax.ShapeDtypeStruct((b, s, d), F32),
        compiler_params=pltpu.CompilerParams(
            dimension_semantics=("parallel", "parallel"), vmem_limit_bytes=VMEM_LIMIT
        ),
        name="post",
    )(h, attn, cb, u, u, u, p, conv_w, g_conv_out, w_out, g_ffn, w_gate, w_up, w_down,
      g_ple, w_ple_gate, w_ple_proj, g_final)


def kernel(x, p, g_mix, w_in, rpb, conv_w, g_attn_out, g_conv_out, w_out, g_ffn, w_gate, w_up,
           w_down, g_ple, w_ple_gate, w_ple_proj, g_final):
    depth = w_in.shape[0]
    row = lambda g: g.reshape(g.shape[0], 1, g.shape[1])
    g_mix, g_attn_out, g_conv_out, g_ffn, g_ple = map(row, (g_mix, g_attn_out, g_conv_out, g_ffn, g_ple))
    g_final = g_final.reshape(1, -1)
    w_in, w_out, w_gate, w_up, w_down, w_ple_gate, w_ple_proj = (
        w.astype(BF16) for w in (w_in, w_out, w_gate, w_up, w_down, w_ple_gate, w_ple_proj)
    )
    bias_a, bias_b = _bias_tables(rpb)

    h = x
    for layer in range(depth):
        qkv, cb, u = _proj_in(h, g_mix, w_in, layer)
        attn = _attention(qkv, bias_a, bias_b, g_attn_out, layer)
        h = _post(h, attn, cb, u, p, conv_w, g_conv_out, w_out, g_ffn, w_gate, w_up, w_down,
                  g_ple, w_ple_gate, w_ple_proj, g_final, layer, layer == depth - 1)
    return h
```

```python
import functools

import jax
import jax.numpy as jnp
from jax import lax
from jax.experimental import pallas as pl
from jax.experimental.pallas import tpu as pltpu

F32 = jnp.float32
BF16 = jnp.bfloat16

D_MODEL = 1024
GRID_W = 64
ATTN_W = 512
CONV_W = 512
HEAD_DIM = 64
N_HEADS = ATTN_W // HEAD_DIM
WIN_H = 8
WIN_W = 16
CONV_K = 3
D_FF = 2816
PLE_DIM = 256
EPS = 1e-6

HEADS_PER_GROUP = 4
GROUP_W = HEADS_PER_GROUP * HEAD_DIM
N_GROUPS = N_HEADS // HEADS_PER_GROUP
WIN_KEYS = WIN_H * GRID_W
SLAB = WIN_W
N_SLABS = GRID_W // SLAB
SLAB_ROWS = HEADS_PER_GROUP * SLAB
HALF_KEYS = WIN_KEYS // 2
HALF_ROWS = (N_SLABS - 1) * SLAB_ROWS
MASK_VALUE = -1e30
LOG2E = 1.4426950408889634
Q_SCALE = HEAD_DIM ** -0.5 * LOG2E

TM_PROJ = 1024
PAIR_UNROLL = 2
TM_POST = 512
HALO = 8
LANES = 128
VMEM_LIMIT = 60000 * 1024


def _rms(x, g):
    ms = jnp.mean(x * x, axis=-1, keepdims=True)
    return x * lax.rsqrt(ms + EPS) * g


def _proj_in_kernel(h_ref, g_ref, w_ref, qkv_ref, cb_ref, u_ref):
    hn = _rms(h_ref[0], g_ref[...]).astype(BF16)
    q = jnp.dot(hn, w_ref[:, :ATTN_W], preferred_element_type=F32) * Q_SCALE
    qkv_ref[0, :, :ATTN_W] = q.astype(BF16)
    kv = jnp.dot(hn, w_ref[:, ATTN_W : 3 * ATTN_W], preferred_element_type=F32)
    qkv_ref[0, :, ATTN_W:] = kv.astype(BF16)
    c0 = 3 * ATTN_W
    cb_ref[0] = jnp.dot(hn, w_ref[:, c0 : c0 + CONV_W], preferred_element_type=F32)
    cc = jnp.dot(hn, w_ref[:, c0 + CONV_W : c0 + 2 * CONV_W], preferred_element_type=F32)
    cu = jnp.dot(hn, w_ref[:, c0 + 2 * CONV_W : c0 + 3 * CONV_W], preferred_element_type=F32)
    u_ref[0] = cc * cu


def _proj_in(h, g_mix, w_in, layer):
    b, s, d = h.shape
    n_in = w_in.shape[-1]
    tm = TM_PROJ
    return pl.pallas_call(
        _proj_in_kernel,
        grid=(b, s // tm),
        in_specs=[
            pl.BlockSpec((1, tm, d), lambda i, j: (i, j, 0)),
            pl.BlockSpec((None, 1, d), lambda i, j: (layer, 0, 0)),
            pl.BlockSpec((None, d, n_in), lambda i, j: (layer, 0, 0), pipeline_mode=pl.Buffered(1)),
        ],
        out_specs=[
            pl.BlockSpec((1, tm, 3 * ATTN_W), lambda i, j: (i, j, 0)),
            pl.BlockSpec((1, tm, CONV_W), lambda i, j: (i, j, 0)),
            pl.BlockSpec((1, tm, CONV_W), lambda i, j: (i, j, 0)),
        ],
        out_shape=[
            jax.ShapeDtypeStruct((b, s, 3 * ATTN_W), BF16),
            jax.ShapeDtypeStruct((b, s, CONV_W), F32),
            jax.ShapeDtypeStruct((b, s, CONV_W), F32),
        ],
        compiler_params=pltpu.CompilerParams(
            dimension_semantics=("parallel", "parallel"), vmem_limit_bytes=VMEM_LIMIT
        ),
        name="proj_in",
    )(h, g_mix, w_in)


def _bias_tables(rpb):
    depth = rpb.shape[0]
    cols = jnp.arange(GRID_W)
    col_start = jnp.clip(cols - WIN_W // 2, 0, GRID_W - WIN_W)
    kc = jnp.arange(GRID_W)
    valid = (kc[None, :] >= col_start[:, None]) & (kc[None, :] < col_start[:, None] + WIN_W)
    off = jnp.clip(kc[None, :] - cols[:, None] + (WIN_W - 1), 0, 2 * WIN_W - 2)
    colpart = rpb[:, :, :, off] * LOG2E
    colpart = jnp.where(valid[None, None, None], colpart, MASK_VALUE)
    n_off = 2 * WIN_H - 1
    colpart = colpart.reshape(depth, N_GROUPS, HEADS_PER_GROUP, n_off, N_SLABS, SLAB, N_SLABS, SLAB)
    colpart = colpart.transpose(0, 1, 6, 4, 2, 5, 3, 7)
    colpart = colpart.reshape(depth, N_GROUPS, N_SLABS, N_SLABS * SLAB_ROWS, n_off, SLAB)
    tiles = [colpart[..., s : s + WIN_H, :].reshape(depth, N_GROUPS, N_SLABS, N_SLABS * SLAB_ROWS, LANES)
             for s in range(WIN_H)]
    return jnp.stack(tiles, axis=3).astype(F32)


_ACTIVE_TILES = (
    ((0, 0), (0, 1)),
    ((0, 0), (0, 1), (1, 0)),
    ((0, 1), (1, 0), (1, 1)),
    ((1, 0), (1, 1)),
)


def _attention_kernel(q_ref, k_ref, v_ref, bias_ref, g_ref, o_ref,
                      sa_scr, sb_scr, pa_scr, pb_scr, l_scr):
    n_rows = k_ref.shape[1] // GRID_W
    lane_head = lax.broadcasted_iota(jnp.int32, (SLAB, GROUP_W), 1) // HEAD_DIM
    s_scr = (sa_scr, sb_scr)
    p_scr = (pa_scr, pb_scr)

    pa_scr[:, :, 2 * SLAB_ROWS :, :LANES] = jnp.zeros((2, N_GROUPS, SLAB_ROWS, LANES), BF16)
    pb_scr[:, :, :SLAB_ROWS, LANES:] = jnp.zeros((2, N_GROUPS, SLAB_ROWS, LANES), BF16)

    def window(r):
        r0 = jnp.clip(r - WIN_H // 2, 0, n_rows - WIN_H)
        return r - r0, r0 * GRID_W

    def half_window(ref, key0, lanes, half):
        chunks = []
        for kb in (2 * half, 2 * half + 1):
            for kk in range(WIN_H):
                start = pl.multiple_of(key0 + (kk * GRID_W + kb * SLAB), SLAB)
                chunks.append(ref[0, pl.ds(start, SLAB), lanes])
        return jnp.concatenate(chunks, axis=0)

    def logits(r, slot):
        dlt, key0 = window(r)
        q_row = q_ref[0, pl.ds(pl.multiple_of(r * GRID_W, GRID_W), GRID_W), :]
        for g in range(N_GROUPS):
            lanes = slice(g * GROUP_W, (g + 1) * GROUP_W)
            q4 = q_row[:, lanes]
            lhs = jnp.concatenate(
                [jnp.where(lane_head == h, q4[a * SLAB : (a + 1) * SLAB], jnp.zeros((SLAB, GROUP_W), BF16))
                 for a in range(N_SLABS) for h in range(HEADS_PER_GROUP)],
                axis=0,
            )
            for half in range(2):
                rows = slice(half * SLAB_ROWS, half * SLAB_ROWS + HALF_ROWS)
                k_half = half_window(k_ref, key0, lanes, half)
                sc = lax.dot_general(lhs[rows], k_half, (((1,), (1,)), ((), ())),
                                     preferred_element_type=F32)
                bias = jnp.concatenate(
                    [bias_ref[g, 2 * half + blk, WIN_H - 1 - dlt, rows, :] for blk in range(2)], axis=1)
                s_scr[half][slot, g] = sc + bias

    def softmax(slot):
        for g in range(N_GROUPS):
            for a in range(N_SLABS):
                tiles_at = [
                    (half, slice((a - half) * SLAB_ROWS, (a - half + 1) * SLAB_ROWS),
                     slice(blk * LANES, (blk + 1) * LANES))
                    for half, blk in _ACTIVE_TILES[a]
                ]
                tiles = [s_scr[half][slot, g, rows, cols] for half, rows, cols in tiles_at]
                m = jnp.max(functools.reduce(jnp.maximum, tiles), axis=-1, keepdims=True)
                es = [jnp.exp2(t - m) for t in tiles]
                l = jnp.sum(functools.reduce(jnp.add, es), axis=-1, keepdims=True)
                l_scr[slot, g, a * SLAB_ROWS : (a + 1) * SLAB_ROWS, :] = jnp.broadcast_to(l, (SLAB_ROWS, LANES))
                for e, (half, rows, cols) in zip(es, tiles_at):
                    p_scr[half][slot, g, rows, cols] = e.astype(BF16)

    def output(r, slot):
        _, key0 = window(r)
        groups = []
        for g in range(N_GROUPS):
            lanes = slice(g * GROUP_W, (g + 1) * GROUP_W)
            oa = jnp.dot(pa_scr[slot, g], half_window(v_ref, key0, lanes, 0), preferred_element_type=F32)
            ob = jnp.dot(pb_scr[slot, g], half_window(v_ref, key0, lanes, 1), preferred_element_type=F32)
            inv_l = 1.0 / l_scr[slot, g]
            slabs = []
            for a in range(N_SLABS):
                parts = []
                if a < N_SLABS - 1:
                    parts.append(oa[a * SLAB_ROWS : (a + 1) * SLAB_ROWS])
                if a > 0:
                    parts.append(ob[(a - 1) * SLAB_ROWS : a * SLAB_ROWS])
                inv = inv_l[a * SLAB_ROWS : (a + 1) * SLAB_ROWS]
                o = functools.reduce(jnp.add, parts) * jnp.concatenate([inv] * (GROUP_W // LANES), axis=1)
                o16 = jnp.zeros((SLAB, GROUP_W), F32)
                for h in range(HEADS_PER_GROUP):
                    o16 = o16 + jnp.where(lane_head == h, o[h * SLAB : (h + 1) * SLAB], 0.0)
                slabs.append(o16)
            groups.append(jnp.concatenate(slabs, axis=0))
        attn = jnp.concatenate(groups, axis=1)
        tok = pl.multiple_of(r * GRID_W, GRID_W)
        o_ref[0, pl.ds(tok, GRID_W), :] = _rms(attn, g_ref[...]).astype(BF16)

    def pair(j, carry):
        i = 2 * j
        logits(i, 0)
        softmax(1)
        output(i - 2, 0)
        logits(i + 1, 1)
        softmax(0)
        output(i - 1, 1)
        return carry

    logits(0, 0)
    logits(1, 1)
    softmax(0)
    lax.fori_loop(1, n_rows // 2, pair, 0, unroll=PAIR_UNROLL)
    softmax(1)
    output(n_rows - 2, 0)
    output(n_rows - 1, 1)


def _attention(qkv, bias_tbl, g_attn_out, layer):
    b, s, _ = qkv.shape
    image = lambda col: pl.BlockSpec((1, s, ATTN_W), lambda i: (i, 0, col))
    table = pl.BlockSpec(
        (None,) + bias_tbl.shape[1:],
        lambda i: (layer, 0, 0, 0, 0, 0),
        pipeline_mode=pl.Buffered(1),
    )
    half_scores = lambda dtype: pltpu.VMEM((2, N_GROUPS, HALF_ROWS, HALF_KEYS), dtype)
    return pl.pallas_call(
        _attention_kernel,
        grid=(b,),
        in_specs=[
            image(0),
            image(1),
            image(2),
            table,
            pl.BlockSpec((None, 1, ATTN_W), lambda i: (layer, 0, 0)),
        ],
        out_specs=image(0),
        out_shape=jax.ShapeDtypeStruct((b, s, ATTN_W), BF16),
        scratch_shapes=[
            half_scores(F32),
            half_scores(F32),
            half_scores(BF16),
            half_scores(BF16),
            pltpu.VMEM((2, N_GROUPS, N_SLABS * SLAB_ROWS, LANES), F32),
        ],
        compiler_params=pltpu.CompilerParams(
            dimension_semantics=("parallel",), vmem_limit_bytes=VMEM_LIMIT
        ),
        name="attention",
    )(qkv, qkv, qkv, bias_tbl, g_attn_out)


def _post_kernel(
    h_ref, attn_ref, cb_ref, u_ref, up_ref, un_ref, p_ref,
    cw_ref, gconv_ref, wout_ref, gffn_ref, wgate_ref, wup_ref, wdown_ref,
    gple_ref, wpg_ref, wpp_ref, gfin_ref, o_ref, *, last,
):
    j = pl.program_id(1)
    nj = pl.num_programs(1)
    tm = h_ref.shape[1]

    u = u_ref[0]
    row = lax.broadcasted_iota(jnp.int32, u.shape, 0)
    prev_row = jnp.where(j == 0, 0.0, up_ref[0, HALO - 1 : HALO, :])
    next_row = jnp.where(j == nj - 1, 0.0, un_ref[0, 0:1, :])
    u_prev = jnp.where(row == 0, prev_row, pltpu.roll(u, 1, 0))
    u_next = jnp.where(row == tm - 1, next_row, pltpu.roll(u, tm - 1, 0))
    conv = cb_ref[0] * (cw_ref[0:1, :] * u_prev + cw_ref[1:2, :] * u + cw_ref[2:3, :] * u_next)
    conv_n = _rms(conv, gconv_ref[...]).astype(BF16)

    mixed = jnp.concatenate([attn_ref[0], conv_n], axis=1)
    h1 = h_ref[0] + jnp.dot(mixed, wout_ref[...], preferred_element_type=F32)

    hn = _rms(h1, gffn_ref[...]).astype(BF16)
    gate = jnp.dot(hn, wgate_ref[...], preferred_element_type=F32)
    up = jnp.dot(hn, wup_ref[...], preferred_element_type=F32)
    act = (gate * jax.nn.sigmoid(gate) * up).astype(BF16)
    h2 = h1 + jnp.dot(act, wdown_ref[...], preferred_element_type=F32)

    hn = _rms(h2, gple_ref[...]).astype(BF16)
    gate = jax.nn.sigmoid(jnp.dot(hn, wpg_ref[...], preferred_element_type=F32))
    proj = jnp.dot(p_ref[0].astype(BF16), wpp_ref[...], preferred_element_type=F32)
    h3 = h2 + gate * proj
    if last:
        h3 = _rms(h3, gfin_ref[...])
    o_ref[0] = h3


def _post(h, attn, cb, u, p, conv_w, g_conv_out, w_out, g_ffn, w_gate, w_up, w_down,
          g_ple, w_ple_gate, w_ple_proj, g_final, layer, last):
    b, s, d = h.shape
    tm = TM_POST
    hb = tm // HALO
    n_halo = s // HALO

    def tile(width):
        return pl.BlockSpec((1, tm, width), lambda i, j: (i, j, 0))

    def resident(shape):
        zeros = (0,) * len(shape)
        return pl.BlockSpec((None,) + shape, lambda i, j: (layer,) + zeros,
                            pipeline_mode=pl.Buffered(1))

    in_specs = [
        tile(d),
        tile(ATTN_W),
        tile(CONV_W),
        tile(CONV_W),
        pl.BlockSpec((1, HALO, CONV_W), lambda i, j: (i, jnp.maximum(j * hb - 1, 0), 0)),
        pl.BlockSpec((1, HALO, CONV_W), lambda i, j: (i, jnp.minimum((j + 1) * hb, n_halo - 1), 0)),
        pl.BlockSpec((None, 1, tm, PLE_DIM), lambda i, j: (layer, i, j, 0)),
        resident((CONV_K, CONV_W)),
        resident((1, CONV_W)),
        resident((d, d)),
        resident((1, d)),
        resident((d, D_FF)),
        resident((d, D_FF)),
        resident((D_FF, d)),
        resident((1, d)),
        resident((d, d)),
        resident((PLE_DIM, d)),
        pl.BlockSpec((1, d), lambda i, j: (0, 0)),
    ]
    return pl.pallas_call(
        functools.partial(_post_kernel, last=last),
        grid=(b, s // tm),
        in_specs=in_specs,
        out_specs=tile(d),
        out_shape=jax.ShapeDtypeStruct((b, s, d), F32),
        compiler_params=pltpu.CompilerParams(
            dimension_semantics=("parallel", "parallel"), vmem_limit_bytes=VMEM_LIMIT
        ),
        name="post",
    )(h, attn, cb, u, u, u, p, conv_w, g_conv_out, w_out, g_ffn, w_gate, w_up, w_down,
      g_ple, w_ple_gate, w_ple_proj, g_final)


def kernel(x, p, g_mix, w_in, rpb, conv_w, g_attn_out, g_conv_out, w_out, g_ffn, w_gate, w_up,
           w_down, g_ple, w_ple_gate, w_ple_proj, g_final):
    depth = w_in.shape[0]
    row = lambda g: g.reshape(g.shape[0], 1, g.shape[1])
    g_mix, g_attn_out, g_conv_out, g_ffn, g_ple = map(row, (g_mix, g_attn_out, g_conv_out, g_ffn, g_ple))
    g_final = g_final.reshape(1, -1)
    w_in, w_out, w_gate, w_up, w_down, w_ple_gate, w_ple_proj = (
        w.astype(BF16) for w in (w_in, w_out, w_gate, w_up, w_down, w_ple_gate, w_ple_proj)
    )
    bias_tbl = _bias_tables(rpb)

    h = x
    for layer in range(depth):
        qkv, cb, u = _proj_in(h, g_mix, w_in, layer)
        attn = _attention(qkv, bias_tbl, g_attn_out, layer)
        h = _post(h, attn, cb, u, p, conv_w, g_conv_out, w_out, g_ffn, w_gate, w_up, w_down,
                  g_ple, w_ple_gate, w_ple_proj, g_final, layer, layer == depth - 1)
    return h
```

```python
import functools

import jax
import jax.numpy as jnp
import numpy as np
from jax import lax
from jax.experimental import pallas as pl
from jax.experimental.pallas import tpu as pltpu

F32 = jnp.float32
BF16 = jnp.bfloat16

D_MODEL = 1024
GRID_W = 64
ATTN_W = 512
CONV_W = 512
HEAD_DIM = 64
N_HEADS = ATTN_W // HEAD_DIM
WIN_H = 8
WIN_W = 16
CONV_K = 3
D_FF = 2816
PLE_DIM = 256
EPS = 1e-6

HEADS_PER_GROUP = 4
GROUP_W = HEADS_PER_GROUP * HEAD_DIM
N_GROUPS = N_HEADS // HEADS_PER_GROUP
WIN_KEYS = WIN_H * GRID_W
SLAB = WIN_W
N_SLABS = GRID_W // SLAB
SLAB_ROWS = HEADS_PER_GROUP * SLAB
HALF_KEYS = WIN_KEYS // 2
HALF_ROWS = (N_SLABS - 1) * SLAB_ROWS
MASK_VALUE = -1e30
LOG2E = 1.4426950408889634
Q_SCALE = HEAD_DIM ** -0.5 * LOG2E

TM_PROJ = 1024
PAIR_UNROLL = 3
TM_POST = 512
HALO = 8
LANES = 128
VMEM_LIMIT = 60000 * 1024


def _rms(x, g):
    ms = jnp.mean(x * x, axis=-1, keepdims=True)
    return x * lax.rsqrt(ms + EPS) * g


def _proj_in_kernel(h_ref, g_ref, w_ref, qkv_ref, cb_ref, u_ref):
    hn = _rms(h_ref[0], g_ref[...]).astype(BF16)
    q = jnp.dot(hn, w_ref[:, :ATTN_W], preferred_element_type=F32) * Q_SCALE
    qkv_ref[0, :, :ATTN_W] = q.astype(BF16)
    kv = jnp.dot(hn, w_ref[:, ATTN_W : 3 * ATTN_W], preferred_element_type=F32)
    qkv_ref[0, :, ATTN_W:] = kv.astype(BF16)
    c0 = 3 * ATTN_W
    cb_ref[0] = jnp.dot(hn, w_ref[:, c0 : c0 + CONV_W], preferred_element_type=F32)
    cc = jnp.dot(hn, w_ref[:, c0 + CONV_W : c0 + 2 * CONV_W], preferred_element_type=F32)
    cu = jnp.dot(hn, w_ref[:, c0 + 2 * CONV_W : c0 + 3 * CONV_W], preferred_element_type=F32)
    u_ref[0] = cc * cu


def _proj_in(h, g_mix, w_in, layer):
    b, s, d = h.shape
    n_in = w_in.shape[-1]
    tm = TM_PROJ
    return pl.pallas_call(
        _proj_in_kernel,
        grid=(b, s // tm),
        in_specs=[
            pl.BlockSpec((1, tm, d), lambda i, j: (i, j, 0)),
            pl.BlockSpec((None, 1, d), lambda i, j: (layer, 0, 0)),
            pl.BlockSpec((None, d, n_in), lambda i, j: (layer, 0, 0), pipeline_mode=pl.Buffered(1)),
        ],
        out_specs=[
            pl.BlockSpec((1, tm, 3 * ATTN_W), lambda i, j: (i, j, 0)),
            pl.BlockSpec((1, tm, CONV_W), lambda i, j: (i, j, 0)),
            pl.BlockSpec((1, tm, CONV_W), lambda i, j: (i, j, 0)),
        ],
        out_shape=[
            jax.ShapeDtypeStruct((b, s, 3 * ATTN_W), BF16),
            jax.ShapeDtypeStruct((b, s, CONV_W), F32),
            jax.ShapeDtypeStruct((b, s, CONV_W), F32),
        ],
        compiler_params=pltpu.CompilerParams(
            dimension_semantics=("parallel", "parallel"), vmem_limit_bytes=VMEM_LIMIT
        ),
        name="proj_in",
    )(h, g_mix, w_in)


def _bias_tables(rpb):
    depth = rpb.shape[0]
    n_off = 2 * WIN_H - 1
    n_col_off = 2 * WIN_W - 1
    cols = np.arange(GRID_W)
    col_start = np.clip(cols - WIN_W // 2, 0, GRID_W - WIN_W)
    kc = np.arange(GRID_W)
    valid = (kc[None, :] >= col_start[:, None]) & (kc[None, :] < col_start[:, None] + WIN_W)
    off = np.where(valid, kc[None, :] - cols[:, None] + (WIN_W - 1), n_col_off)
    masked_col = jnp.full(rpb.shape[:-1] + (1,), MASK_VALUE, F32)
    colpart = jnp.concatenate([rpb * LOG2E, masked_col], axis=-1)[:, :, :, off]
    colpart = colpart.reshape(depth, N_GROUPS, HEADS_PER_GROUP, n_off, N_SLABS, SLAB, N_SLABS, SLAB)
    colpart = colpart.transpose(0, 1, 6, 4, 2, 5, 3, 7)
    colpart = colpart.reshape(depth, N_GROUPS, N_SLABS, N_SLABS * SLAB_ROWS, n_off, SLAB)
    tiles = [colpart[..., s : s + WIN_H, :].reshape(depth, N_GROUPS, N_SLABS, N_SLABS * SLAB_ROWS, LANES)
             for s in range(WIN_H)]
    return jnp.stack(tiles, axis=3).astype(F32)


_ACTIVE_TILES = (
    ((0, 0), (0, 1)),
    ((0, 0), (0, 1), (1, 0)),
    ((0, 1), (1, 0), (1, 1)),
    ((1, 0), (1, 1)),
)


def _attention_kernel(q_ref, k_ref, v_ref, bias_ref, g_ref, o_ref,
                      sa_scr, sb_scr, pa_scr, pb_scr, l_scr):
    n_rows = k_ref.shape[1] // GRID_W
    lane_head = lax.broadcasted_iota(jnp.int32, (SLAB, GROUP_W), 1) // HEAD_DIM
    s_scr = (sa_scr, sb_scr)
    p_scr = (pa_scr, pb_scr)

    pa_scr[:, :, 2 * SLAB_ROWS :, :LANES] = jnp.zeros((2, N_GROUPS, SLAB_ROWS, LANES), BF16)
    pb_scr[:, :, :SLAB_ROWS, LANES:] = jnp.zeros((2, N_GROUPS, SLAB_ROWS, LANES), BF16)

    def window(r):
        r0 = jnp.clip(r - WIN_H // 2, 0, n_rows - WIN_H)
        return r - r0, r0 * GRID_W

    def half_window(ref, key0, lanes, half):
        win = ref.at[0, pl.ds(pl.multiple_of(key0, GRID_W), WIN_KEYS)]
        chunks = []
        for kb in (2 * half, 2 * half + 1):
            for kk in range(WIN_H):
                start = kk * GRID_W + kb * SLAB
                chunks.append(win[start : start + SLAB, lanes])
        return jnp.concatenate(chunks, axis=0)

    def logits(r, slot):
        dlt, key0 = window(r)
        q_row = q_ref[0, pl.ds(pl.multiple_of(r * GRID_W, GRID_W), GRID_W), :]
        for g in range(N_GROUPS):
            lanes = slice(g * GROUP_W, (g + 1) * GROUP_W)
            q4 = q_row[:, lanes]
            lhs = jnp.concatenate(
                [jnp.where(lane_head == h, q4[a * SLAB : (a + 1) * SLAB], jnp.zeros((SLAB, GROUP_W), BF16))
                 for a in range(N_SLABS) for h in range(HEADS_PER_GROUP)],
                axis=0,
            )
            for half in range(2):
                rows = slice(half * SLAB_ROWS, half * SLAB_ROWS + HALF_ROWS)
                k_half = half_window(k_ref, key0, lanes, half)
                sc = lax.dot_general(lhs[rows], k_half, (((1,), (1,)), ((), ())),
                                     preferred_element_type=F32)
                bias = jnp.concatenate(
                    [bias_ref[g, 2 * half + blk, WIN_H - 1 - dlt, rows, :] for blk in range(2)], axis=1)
                s_scr[half][slot, g] = sc + bias

    def softmax(slot):
        for g in range(N_GROUPS):
            for a in range(N_SLABS):
                tiles_at = [
                    (half, slice((a - half) * SLAB_ROWS, (a - half + 1) * SLAB_ROWS),
                     slice(blk * LANES, (blk + 1) * LANES))
                    for half, blk in _ACTIVE_TILES[a]
                ]
                tiles = [s_scr[half][slot, g, rows, cols] for half, rows, cols in tiles_at]
                m = jnp.max(functools.reduce(jnp.maximum, tiles), axis=-1, keepdims=True)
                es = [jnp.exp2(t - m) for t in tiles]
                l = jnp.sum(functools.reduce(jnp.add, es), axis=-1, keepdims=True)
                l_scr[slot, g, a * SLAB_ROWS : (a + 1) * SLAB_ROWS, :] = jnp.broadcast_to(l, (SLAB_ROWS, LANES))
                for e, (half, rows, cols) in zip(es, tiles_at):
                    p_scr[half][slot, g, rows, cols] = e.astype(BF16)

    def output(r, slot):
        _, key0 = window(r)
        groups = []
        for g in range(N_GROUPS):
            lanes = slice(g * GROUP_W, (g + 1) * GROUP_W)
            oa = jnp.dot(pa_scr[slot, g], half_window(v_ref, key0, lanes, 0), preferred_element_type=F32)
            ob = jnp.dot(pb_scr[slot, g], half_window(v_ref, key0, lanes, 1), preferred_element_type=F32)
            inv_l = 1.0 / l_scr[slot, g]
            slabs = []
            for a in range(N_SLABS):
                parts = []
                if a < N_SLABS - 1:
                    parts.append(oa[a * SLAB_ROWS : (a + 1) * SLAB_ROWS])
                if a > 0:
                    parts.append(ob[(a - 1) * SLAB_ROWS : a * SLAB_ROWS])
                inv = inv_l[a * SLAB_ROWS : (a + 1) * SLAB_ROWS]
                o = functools.reduce(jnp.add, parts) * jnp.concatenate([inv] * (GROUP_W // LANES), axis=1)
                o16 = jnp.zeros((SLAB, GROUP_W), F32)
                for h in range(HEADS_PER_GROUP):
                    o16 = o16 + jnp.where(lane_head == h, o[h * SLAB : (h + 1) * SLAB], 0.0)
                slabs.append(o16)
            groups.append(jnp.concatenate(slabs, axis=0))
        attn = jnp.concatenate(groups, axis=1)
        tok = pl.multiple_of(r * GRID_W, GRID_W)
        o_ref[0, pl.ds(tok, GRID_W), :] = _rms(attn, g_ref[...]).astype(BF16)

    def pair(j, carry):
        i = 2 * j
        logits(i, 0)
        softmax(1)
        output(i - 2, 0)
        logits(i + 1, 1)
        softmax(0)
        output(i - 1, 1)
        return carry

    logits(0, 0)
    logits(1, 1)
    softmax(0)
    lax.fori_loop(1, n_rows // 2, pair, 0, unroll=PAIR_UNROLL)
    softmax(1)
    output(n_rows - 2, 0)
    output(n_rows - 1, 1)


def _attention(qkv, bias_tbl, g_attn_out, layer):
    b, s, _ = qkv.shape
    image = lambda col: pl.BlockSpec((1, s, ATTN_W), lambda i: (i, 0, col))
    table = pl.BlockSpec(
        (None,) + bias_tbl.shape[1:],
        lambda i: (layer, 0, 0, 0, 0, 0),
        pipeline_mode=pl.Buffered(1),
    )
    half_scores = lambda dtype: pltpu.VMEM((2, N_GROUPS, HALF_ROWS, HALF_KEYS), dtype)
    return pl.pallas_call(
        _attention_kernel,
        grid=(b,),
        in_specs=[
            image(0),
            image(1),
            image(2),
            table,
            pl.BlockSpec((None, 1, ATTN_W), lambda i: (layer, 0, 0)),
        ],
        out_specs=image(0),
        out_shape=jax.ShapeDtypeStruct((b, s, ATTN_W), BF16),
        scratch_shapes=[
            half_scores(F32),
            half_scores(F32),
            half_scores(BF16),
            half_scores(BF16),
            pltpu.VMEM((2, N_GROUPS, N_SLABS * SLAB_ROWS, LANES), F32),
        ],
        compiler_params=pltpu.CompilerParams(
            dimension_semantics=("parallel",), vmem_limit_bytes=VMEM_LIMIT
        ),
        name="attention",
    )(qkv, qkv, qkv, bias_tbl, g_attn_out)


def _post_kernel(
    h_ref, attn_ref, cb_ref, u_ref, up_ref, un_ref, p_ref,
    cw_ref, gconv_ref, wout_ref, gffn_ref, wgate_ref, wup_ref, wdown_ref,
    gple_ref, wpg_ref, wpp_ref, gfin_ref, o_ref, *, last,
):
    j = pl.program_id(1)
    nj = pl.num_programs(1)
    tm = h_ref.shape[1]

    u = u_ref[0]
    row = lax.broadcasted_iota(jnp.int32, u.shape, 0)
    prev_row = jnp.where(j == 0, 0.0, up_ref[0, HALO - 1 : HALO, :])
    next_row = jnp.where(j == nj - 1, 0.0, un_ref[0, 0:1, :])
    u_prev = jnp.where(row == 0, prev_row, pltpu.roll(u, 1, 0))
    u_next = jnp.where(row == tm - 1, next_row, pltpu.roll(u, tm - 1, 0))
    conv = cb_ref[0] * (cw_ref[0:1, :] * u_prev + cw_ref[1:2, :] * u + cw_ref[2:3, :] * u_next)
    conv_n = _rms(conv, gconv_ref[...]).astype(BF16)

    mixed = jnp.concatenate([attn_ref[0], conv_n], axis=1)
    h1 = h_ref[0] + jnp.dot(mixed, wout_ref[...], preferred_element_type=F32)

    hn = _rms(h1, gffn_ref[...]).astype(BF16)
    gate = jnp.dot(hn, wgate_ref[...], preferred_element_type=F32)
    up = jnp.dot(hn, wup_ref[...], preferred_element_type=F32)
    act = (gate * jax.nn.sigmoid(gate) * up).astype(BF16)
    h2 = h1 + jnp.dot(act, wdown_ref[...], preferred_element_type=F32)

    hn = _rms(h2, gple_ref[...]).astype(BF16)
    gate = jax.nn.sigmoid(jnp.dot(hn, wpg_ref[...], preferred_element_type=F32))
    proj = jnp.dot(p_ref[0].astype(BF16), wpp_ref[...], preferred_element_type=F32)
    h3 = h2 + gate * proj
    if last:
        h3 = _rms(h3, gfin_ref[...])
    o_ref[0] = h3


def _post(h, attn, cb, u, p, conv_w, g_conv_out, w_out, g_ffn, w_gate, w_up, w_down,
          g_ple, w_ple_gate, w_ple_proj, g_final, layer, last):
    b, s, d = h.shape
    tm = TM_POST
    hb = tm // HALO
    n_halo = s // HALO

    def tile(width):
        return pl.BlockSpec((1, tm, width), lambda i, j: (i, j, 0))

    def resident(shape):
        zeros = (0,) * len(shape)
        return pl.BlockSpec((None,) + shape, lambda i, j: (layer,) + zeros,
                            pipeline_mode=pl.Buffered(1))

    in_specs = [
        tile(d),
        tile(ATTN_W),
        tile(CONV_W),
        tile(CONV_W),
        pl.BlockSpec((1, HALO, CONV_W), lambda i, j: (i, jnp.maximum(j * hb - 1, 0), 0)),
        pl.BlockSpec((1, HALO, CONV_W), lambda i, j: (i, jnp.minimum((j + 1) * hb, n_halo - 1), 0)),
        pl.BlockSpec((None, 1, tm, PLE_DIM), lambda i, j: (layer, i, j, 0)),
        resident((CONV_K, CONV_W)),
        resident((1, CONV_W)),
        resident((d, d)),
        resident((1, d)),
        resident((d, D_FF)),
        resident((d, D_FF)),
        resident((D_FF, d)),
        resident((1, d)),
        resident((d, d)),
        resident((PLE_DIM, d)),
        pl.BlockSpec((1, d), lambda i, j: (0, 0)),
    ]
    return pl.pallas_call(
        functools.partial(_post_kernel, last=last),
        grid=(b, s // tm),
        in_specs=in_specs,
        out_specs=tile(d),
        out_shape=jax.ShapeDtypeStruct((b, s, d), F32),
        compiler_params=pltpu.CompilerParams(
            dimension_semantics=("parallel", "parallel"), vmem_limit_bytes=VMEM_LIMIT
        ),
        name="post",
    )(h, attn, cb, u, u, u, p, conv_w, g_conv_out, w_out, g_ffn, w_gate, w_up, w_down,
      g_ple, w_ple_gate, w_ple_proj, g_final)


def kernel(x, p, g_mix, w_in, rpb, conv_w, g_attn_out, g_conv_out, w_out, g_ffn, w_gate, w_up,
           w_down, g_ple, w_ple_gate, w_ple_proj, g_final):
    depth = w_in.shape[0]
    row = lambda g: g.reshape(g.shape[0], 1, g.shape[1])
    g_mix, g_attn_out, g_conv_out, g_ffn, g_ple = map(row, (g_mix, g_attn_out, g_conv_out, g_ffn, g_ple))
    g_final = g_final.reshape(1, -1)
    w_in, w_out, w_gate, w_up, w_down, w_ple_gate, w_ple_proj = (
        w.astype(BF16) for w in (w_in, w_out, w_gate, w_up, w_down, w_ple_gate, w_ple_proj)
    )
    bias_tbl = _bias_tables(rpb)

    h = x
    for layer in range(depth):
        qkv, cb, u = _proj_in(h, g_mix, w_in, layer)
        attn = _attention(qkv, bias_tbl, g_attn_out, layer)
        h = _post(h, attn, cb, u, p, conv_w, g_conv_out, w_out, g_ffn, w_gate, w_up, w_down,
                  g_ple, w_ple_gate, w_ple_proj, g_final, layer, layer == depth - 1)
    return h
```

```python
import functools

import jax
import jax.numpy as jnp
import numpy as np
from jax import lax
from jax.experimental import pallas as pl
from jax.experimental.pallas import tpu as pltpu

F32 = jnp.float32
BF16 = jnp.bfloat16

D_MODEL = 1024
GRID_W = 64
ATTN_W = 512
CONV_W = 512
HEAD_DIM = 64
N_HEADS = ATTN_W // HEAD_DIM
WIN_H = 8
WIN_W = 16
CONV_K = 3
D_FF = 2816
PLE_DIM = 256
EPS = 1e-6

HEADS_PER_GROUP = 4
GROUP_W = HEADS_PER_GROUP * HEAD_DIM
N_GROUPS = N_HEADS // HEADS_PER_GROUP
WIN_KEYS = WIN_H * GRID_W
SLAB = WIN_W
N_SLABS = GRID_W // SLAB
SLAB_ROWS = HEADS_PER_GROUP * SLAB
HALF_KEYS = WIN_KEYS // 2
HALF_ROWS = (N_SLABS - 1) * SLAB_ROWS
MASK_VALUE = -1e30
LOG2E = 1.4426950408889634
Q_SCALE = HEAD_DIM ** -0.5 * LOG2E

TM_PROJ = 1024
PAIR_UNROLL = 3
TM_POST = 512
HALO = 8
LANES = 128
VMEM_LIMIT = 60000 * 1024


def _rms(x, g):
    ms = jnp.mean(x * x, axis=-1, keepdims=True)
    return x * lax.rsqrt(ms + EPS) * g


def _proj_in_kernel(h_ref, g_ref, w_ref, qkv_ref, cb_ref, u_ref):
    hn = _rms(h_ref[0], g_ref[...]).astype(BF16)
    q = jnp.dot(hn, w_ref[:, :ATTN_W], preferred_element_type=F32) * Q_SCALE
    qkv_ref[0, :, :ATTN_W] = q.astype(BF16)
    kv = jnp.dot(hn, w_ref[:, ATTN_W : 3 * ATTN_W], preferred_element_type=F32)
    qkv_ref[0, :, ATTN_W:] = kv.astype(BF16)
    c0 = 3 * ATTN_W
    cb_ref[0] = jnp.dot(hn, w_ref[:, c0 : c0 + CONV_W], preferred_element_type=F32)
    cc = jnp.dot(hn, w_ref[:, c0 + CONV_W : c0 + 2 * CONV_W], preferred_element_type=F32)
    cu = jnp.dot(hn, w_ref[:, c0 + 2 * CONV_W : c0 + 3 * CONV_W], preferred_element_type=F32)
    u_ref[0] = cc * cu


def _proj_in(h, g_mix, w_in, layer):
    b, s, d = h.shape
    n_in = w_in.shape[-1]
    tm = TM_PROJ
    return pl.pallas_call(
        _proj_in_kernel,
        grid=(b, s // tm),
        in_specs=[
            pl.BlockSpec((1, tm, d), lambda i, j: (i, j, 0)),
            pl.BlockSpec((None, 1, d), lambda i, j: (layer, 0, 0)),
            pl.BlockSpec((None, d, n_in), lambda i, j: (layer, 0, 0), pipeline_mode=pl.Buffered(1)),
        ],
        out_specs=[
            pl.BlockSpec((1, tm, 3 * ATTN_W), lambda i, j: (i, j, 0)),
            pl.BlockSpec((1, tm, CONV_W), lambda i, j: (i, j, 0)),
            pl.BlockSpec((1, tm, CONV_W), lambda i, j: (i, j, 0)),
        ],
        out_shape=[
            jax.ShapeDtypeStruct((b, s, 3 * ATTN_W), BF16),
            jax.ShapeDtypeStruct((b, s, CONV_W), F32),
            jax.ShapeDtypeStruct((b, s, CONV_W), F32),
        ],
        compiler_params=pltpu.CompilerParams(
            dimension_semantics=("parallel", "parallel"), vmem_limit_bytes=VMEM_LIMIT
        ),
        name="proj_in",
    )(h, g_mix, w_in)


def _bias_tables(rpb):
    depth = rpb.shape[0]
    n_off = 2 * WIN_H - 1
    n_col_off = 2 * WIN_W - 1
    cols = np.arange(GRID_W)
    col_start = np.clip(cols - WIN_W // 2, 0, GRID_W - WIN_W)
    kc = np.arange(GRID_W)
    valid = (kc[None, :] >= col_start[:, None]) & (kc[None, :] < col_start[:, None] + WIN_W)
    off = np.where(valid, kc[None, :] - cols[:, None] + (WIN_W - 1), n_col_off)
    off = off.reshape(N_SLABS, SLAB, N_SLABS, SLAB).transpose(2, 0, 1, 3)
    flat = np.arange(n_off)[None, None, None, :, None] * (n_col_off + 1) + off[:, :, :, None, :]
    masked_col = jnp.full(rpb.shape[:-1] + (1,), MASK_VALUE, F32)
    table = jnp.concatenate([rpb * LOG2E, masked_col], axis=-1)
    table = table.reshape(depth, N_GROUPS, HEADS_PER_GROUP, n_off * (n_col_off + 1))
    colpart = table[..., flat.reshape(-1)]
    colpart = colpart.reshape(depth, N_GROUPS, HEADS_PER_GROUP, N_SLABS, N_SLABS, SLAB * n_off * SLAB)
    colpart = colpart.transpose(0, 1, 3, 4, 2, 5)
    colpart = colpart.reshape(depth, N_GROUPS, N_SLABS, N_SLABS * SLAB_ROWS, n_off, SLAB)
    tiles = [colpart[..., s : s + WIN_H, :].reshape(depth, N_GROUPS, N_SLABS, N_SLABS * SLAB_ROWS, LANES)
             for s in range(WIN_H)]
    return jnp.stack(tiles, axis=3)


_ACTIVE_TILES = (
    ((0, 0), (0, 1)),
    ((0, 0), (0, 1), (1, 0)),
    ((0, 1), (1, 0), (1, 1)),
    ((1, 0), (1, 1)),
)


def _attention_kernel(q_ref, k_ref, v_ref, bias_ref, g_ref, o_ref,
                      sa_scr, sb_scr, pa_scr, pb_scr, l_scr):
    n_rows = k_ref.shape[1] // GRID_W
    lane_head = lax.broadcasted_iota(jnp.int32, (SLAB, GROUP_W), 1) // HEAD_DIM
    s_scr = (sa_scr, sb_scr)
    p_scr = (pa_scr, pb_scr)

    pa_scr[:, :, 2 * SLAB_ROWS :, :LANES] = jnp.zeros((2, N_GROUPS, SLAB_ROWS, LANES), BF16)
    pb_scr[:, :, :SLAB_ROWS, LANES:] = jnp.zeros((2, N_GROUPS, SLAB_ROWS, LANES), BF16)

    def window(r):
        r0 = jnp.clip(r - WIN_H // 2, 0, n_rows - WIN_H)
        return r - r0, r0 * GRID_W

    def half_window(ref, key0, lanes, half):
        win = ref.at[0, pl.ds(pl.multiple_of(key0, GRID_W), WIN_KEYS)]
        chunks = []
        for kb in (2 * half, 2 * half + 1):
            for kk in range(WIN_H):
                start = kk * GRID_W + kb * SLAB
                chunks.append(win[start : start + SLAB, lanes])
        return jnp.concatenate(chunks, axis=0)

    def logits(r, slot):
        dlt, key0 = window(r)
        q_row = q_ref[0, pl.ds(pl.multiple_of(r * GRID_W, GRID_W), GRID_W), :]
        for g in range(N_GROUPS):
            lanes = slice(g * GROUP_W, (g + 1) * GROUP_W)
            q4 = q_row[:, lanes]
            lhs = jnp.concatenate(
                [jnp.where(lane_head == h, q4[a * SLAB : (a + 1) * SLAB], jnp.zeros((SLAB, GROUP_W), BF16))
                 for a in range(N_SLABS) for h in range(HEADS_PER_GROUP)],
                axis=0,
            )
            for half in range(2):
                rows = slice(half * SLAB_ROWS, half * SLAB_ROWS + HALF_ROWS)
                k_half = half_window(k_ref, key0, lanes, half)
                sc = lax.dot_general(lhs[rows], k_half, (((1,), (1,)), ((), ())),
                                     preferred_element_type=F32)
                bias = jnp.concatenate(
                    [bias_ref[g, 2 * half + blk, WIN_H - 1 - dlt, rows, :] for blk in range(2)], axis=1)
                s_scr[half][slot, g] = sc + bias

    def softmax(slot):
        for g in range(N_GROUPS):
            for a in range(N_SLABS):
                tiles_at = [
                    (half, slice((a - half) * SLAB_ROWS, (a - half + 1) * SLAB_ROWS),
                     slice(blk * LANES, (blk + 1) * LANES))
                    for half, blk in _ACTIVE_TILES[a]
                ]
                tiles = [s_scr[half][slot, g, rows, cols] for half, rows, cols in tiles_at]
                m = jnp.max(functools.reduce(jnp.maximum, tiles), axis=-1, keepdims=True)
                es = [jnp.exp2(t - m) for t in tiles]
                l = jnp.sum(functools.reduce(jnp.add, es), axis=-1, keepdims=True)
                l_scr[slot, g, a * SLAB_ROWS : (a + 1) * SLAB_ROWS, :] = jnp.broadcast_to(l, (SLAB_ROWS, LANES))
                for e, (half, rows, cols) in zip(es, tiles_at):
                    p_scr[half][slot, g, rows, cols] = e.astype(BF16)

    def output(r, slot):
        _, key0 = window(r)
        groups = []
        for g in range(N_GROUPS):
            lanes = slice(g * GROUP_W, (g + 1) * GROUP_W)
            oa = jnp.dot(pa_scr[slot, g], half_window(v_ref, key0, lanes, 0), preferred_element_type=F32)
            ob = jnp.dot(pb_scr[slot, g], half_window(v_ref, key0, lanes, 1), preferred_element_type=F32)
            inv_l = 1.0 / l_scr[slot, g]
            slabs = []
            for a in range(N_SLABS):
                parts = []
                if a < N_SLABS - 1:
                    parts.append(oa[a * SLAB_ROWS : (a + 1) * SLAB_ROWS])
                if a > 0:
                    parts.append(ob[(a - 1) * SLAB_ROWS : a * SLAB_ROWS])
                inv = inv_l[a * SLAB_ROWS : (a + 1) * SLAB_ROWS]
                o = functools.reduce(jnp.add, parts) * jnp.concatenate([inv] * (GROUP_W // LANES), axis=1)
                o16 = jnp.zeros((SLAB, GROUP_W), F32)
                for h in range(HEADS_PER_GROUP):
                    o16 = o16 + jnp.where(lane_head == h, o[h * SLAB : (h + 1) * SLAB], 0.0)
                slabs.append(o16)
            groups.append(jnp.concatenate(slabs, axis=0))
        attn = jnp.concatenate(groups, axis=1)
        tok = pl.multiple_of(r * GRID_W, GRID_W)
        o_ref[0, pl.ds(tok, GRID_W), :] = _rms(attn, g_ref[...]).astype(BF16)

    def pair(j, carry):
        i = 2 * j
        logits(i, 0)
        softmax(1)
        output(i - 2, 0)
        logits(i + 1, 1)
        softmax(0)
        output(i - 1, 1)
        return carry

    logits(0, 0)
    logits(1, 1)
    softmax(0)
    lax.fori_loop(1, n_rows // 2, pair, 0, unroll=PAIR_UNROLL)
    softmax(1)
    output(n_rows - 2, 0)
    output(n_rows - 1, 1)


def _attention(qkv, bias_tbl, g_attn_out, layer):
    b, s, _ = qkv.shape
    image = lambda col: pl.BlockSpec((1, s, ATTN_W), lambda i: (i, 0, col))
    table = pl.BlockSpec(
        (None,) + bias_tbl.shape[1:],
        lambda i: (layer, 0, 0, 0, 0, 0),
        pipeline_mode=pl.Buffered(1),
    )
    half_scores = lambda dtype: pltpu.VMEM((2, N_GROUPS, HALF_ROWS, HALF_KEYS), dtype)
    return pl.pallas_call(
        _attention_kernel,
        grid=(b,),
        in_specs=[
            image(0),
            image(1),
            image(2),
            table,
            pl.BlockSpec((None, 1, ATTN_W), lambda i: (layer, 0, 0)),
        ],
        out_specs=image(0),
        out_shape=jax.ShapeDtypeStruct((b, s, ATTN_W), BF16),
        scratch_shapes=[
            half_scores(F32),
            half_scores(F32),
            half_scores(BF16),
            half_scores(BF16),
            pltpu.VMEM((2, N_GROUPS, N_SLABS * SLAB_ROWS, LANES), F32),
        ],
        compiler_params=pltpu.CompilerParams(
            dimension_semantics=("parallel",), vmem_limit_bytes=VMEM_LIMIT
        ),
        name="attention",
    )(qkv, qkv, qkv, bias_tbl, g_attn_out)


def _post_kernel(
    h_ref, attn_ref, cb_ref, u_ref, up_ref, un_ref, p_ref,
    cw_ref, gconv_ref, wout_ref, gffn_ref, wgate_ref, wup_ref, wdown_ref,
    gple_ref, wpg_ref, wpp_ref, gfin_ref, o_ref, *, last,
):
    j = pl.program_id(1)
    nj = pl.num_programs(1)
    tm = h_ref.shape[1]

    u = u_ref[0]
    row = lax.broadcasted_iota(jnp.int32, u.shape, 0)
    prev_row = jnp.where(j == 0, 0.0, up_ref[0, HALO - 1 : HALO, :])
    next_row = jnp.where(j == nj - 1, 0.0, un_ref[0, 0:1, :])
    u_prev = jnp.where(row == 0, prev_row, pltpu.roll(u, 1, 0))
    u_next = jnp.where(row == tm - 1, next_row, pltpu.roll(u, tm - 1, 0))
    conv = cb_ref[0] * (cw_ref[0:1, :] * u_prev + cw_ref[1:2, :] * u + cw_ref[2:3, :] * u_next)
    conv_n = _rms(conv, gconv_ref[...]).astype(BF16)

    mixed = jnp.concatenate([attn_ref[0], conv_n], axis=1)
    h1 = h_ref[0] + jnp.dot(mixed, wout_ref[...], preferred_element_type=F32)

    hn = _rms(h1, gffn_ref[...]).astype(BF16)
    gate = jnp.dot(hn, wgate_ref[...], preferred_element_type=F32)
    up = jnp.dot(hn, wup_ref[...], preferred_element_type=F32)
    act = (gate * jax.nn.sigmoid(gate) * up).astype(BF16)
    h2 = h1 + jnp.dot(act, wdown_ref[...], preferred_element_type=F32)

    hn = _rms(h2, gple_ref[...]).astype(BF16)
    gate = jax.nn.sigmoid(jnp.dot(hn, wpg_ref[...], preferred_element_type=F32))
    proj = jnp.dot(p_ref[0].astype(BF16), wpp_ref[...], preferred_element_type=F32)
    h3 = h2 + gate * proj
    if last:
        h3 = _rms(h3, gfin_ref[...])
    o_ref[0] = h3


def _post(h, attn, cb, u, p, conv_w, g_conv_out, w_out, g_ffn, w_gate, w_up, w_down,
          g_ple, w_ple_gate, w_ple_proj, g_final, layer, last):
    b, s, d = h.shape
    tm = TM_POST
    hb = tm // HALO
    n_halo = s // HALO

    def tile(width):
        return pl.BlockSpec((1, tm, width), lambda i, j: (i, j, 0))

    def resident(shape):
        zeros = (0,) * len(shape)
        return pl.BlockSpec((None,) + shape, lambda i, j: (layer,) + zeros,
                            pipeline_mode=pl.Buffered(1))

    in_specs = [
        tile(d),
        tile(ATTN_W),
        tile(CONV_W),
        tile(CONV_W),
        pl.BlockSpec((1, HALO, CONV_W), lambda i, j: (i, jnp.maximum(j * hb - 1, 0), 0)),
        pl.BlockSpec((1, HALO, CONV_W), lambda i, j: (i, jnp.minimum((j + 1) * hb, n_halo - 1), 0)),
        pl.BlockSpec((None, 1, tm, PLE_DIM), lambda i, j: (layer, i, j, 0)),
        resident((CONV_K, CONV_W)),
        resident((1, CONV_W)),
        resident((d, d)),
        resident((1, d)),
        resident((d, D_FF)),
        resident((d, D_FF)),
        resident((D_FF, d)),
        resident((1, d)),
        resident((d, d)),
        resident((PLE_DIM, d)),
        pl.BlockSpec((1, d), lambda i, j: (0, 0)),
    ]
    return pl.pallas_call(
        functools.partial(_post_kernel, last=last),
        grid=(b, s // tm),
        in_specs=in_specs,
        out_specs=tile(d),
        out_shape=jax.ShapeDtypeStruct((b, s, d), F32),
        compiler_params=pltpu.CompilerParams(
            dimension_semantics=("parallel", "parallel"), vmem_limit_bytes=VMEM_LIMIT
        ),
        name="post",
    )(h, attn, cb, u, u, u, p, conv_w, g_conv_out, w_out, g_ffn, w_gate, w_up, w_down,
      g_ple, w_ple_gate, w_ple_proj, g_final)


def kernel(x, p, g_mix, w_in, rpb, conv_w, g_attn_out, g_conv_out, w_out, g_ffn, w_gate, w_up,
           w_down, g_ple, w_ple_gate, w_ple_proj, g_final):
    depth = w_in.shape[0]
    row = lambda g: g.reshape(g.shape[0], 1, g.shape[1])
    g_mix, g_attn_out, g_conv_out, g_ffn, g_ple = map(row, (g_mix, g_attn_out, g_conv_out, g_ffn, g_ple))
    g_final = g_final.reshape(1, -1)
    w_in, w_out, w_gate, w_up, w_down, w_ple_gate, w_ple_proj = (
        w.astype(BF16) for w in (w_in, w_out, w_gate, w_up, w_down, w_ple_gate, w_ple_proj)
    )
    bias_tbl = _bias_tables(rpb)

    h = x
    for layer in range(depth):
        qkv, cb, u = _proj_in(h, g_mix, w_in, layer)
        attn = _attention(qkv, bias_tbl, g_attn_out, layer)
        h = _post(h, attn, cb, u, p, conv_w, g_conv_out, w_out, g_ffn, w_gate, w_up, w_down,
                  g_ple, w_ple_gate, w_ple_proj, g_final, layer, layer == depth - 1)
    return h
```

```python
import functools

import jax
import jax.numpy as jnp
import numpy as np
from jax import lax
from jax.experimental import pallas as pl
from jax.experimental.pallas import tpu as pltpu

F32 = jnp.float32
BF16 = jnp.bfloat16

D_MODEL = 1024
GRID_W = 64
ATTN_W = 512
CONV_W = 512
HEAD_DIM = 64
N_HEADS = ATTN_W // HEAD_DIM
WIN_H = 8
WIN_W = 16
CONV_K = 3
D_FF = 2816
PLE_DIM = 256
EPS = 1e-6

HEADS_PER_GROUP = 4
GROUP_W = HEADS_PER_GROUP * HEAD_DIM
N_GROUPS = N_HEADS // HEADS_PER_GROUP
WIN_KEYS = WIN_H * GRID_W
SLAB = WIN_W
N_SLABS = GRID_W // SLAB
SLAB_ROWS = HEADS_PER_GROUP * SLAB
HALF_KEYS = WIN_KEYS // 2
HALF_ROWS = (N_SLABS - 1) * SLAB_ROWS
MASK_VALUE = -1e30
LOG2E = 1.4426950408889634
Q_SCALE = HEAD_DIM ** -0.5 * LOG2E

TM_PROJ = 1024
PAIR_UNROLL = 5
TM_POST = 512
HALO = 8
LANES = 128
VMEM_LIMIT = 60000 * 1024


def _rms(x, g):
    ms = jnp.mean(x * x, axis=-1, keepdims=True)
    return x * lax.rsqrt(ms + EPS) * g


def _proj_in_kernel(h_ref, g_ref, w_ref, qkv_ref, cb_ref, u_ref):
    hn = _rms(h_ref[0], g_ref[...]).astype(BF16)
    q = jnp.dot(hn, w_ref[:, :ATTN_W], preferred_element_type=F32) * Q_SCALE
    qkv_ref[0, :, :ATTN_W] = q.astype(BF16)
    kv = jnp.dot(hn, w_ref[:, ATTN_W : 3 * ATTN_W], preferred_element_type=F32)
    qkv_ref[0, :, ATTN_W:] = kv.astype(BF16)
    c0 = 3 * ATTN_W
    cb_ref[0] = jnp.dot(hn, w_ref[:, c0 : c0 + CONV_W], preferred_element_type=F32)
    cc = jnp.dot(hn, w_ref[:, c0 + CONV_W : c0 + 2 * CONV_W], preferred_element_type=F32)
    cu = jnp.dot(hn, w_ref[:, c0 + 2 * CONV_W : c0 + 3 * CONV_W], preferred_element_type=F32)
    u_ref[0] = cc * cu


def _proj_in(h, g_mix, w_in, layer):
    b, s, d = h.shape
    n_in = w_in.shape[-1]
    tm = TM_PROJ
    return pl.pallas_call(
        _proj_in_kernel,
        grid=(b, s // tm),
        in_specs=[
            pl.BlockSpec((1, tm, d), lambda i, j: (i, j, 0)),
            pl.BlockSpec((None, 1, d), lambda i, j: (layer, 0, 0)),
            pl.BlockSpec((None, d, n_in), lambda i, j: (layer, 0, 0), pipeline_mode=pl.Buffered(1)),
        ],
        out_specs=[
            pl.BlockSpec((1, tm, 3 * ATTN_W), lambda i, j: (i, j, 0)),
            pl.BlockSpec((1, tm, CONV_W), lambda i, j: (i, j, 0)),
            pl.BlockSpec((1, tm, CONV_W), lambda i, j: (i, j, 0)),
        ],
        out_shape=[
            jax.ShapeDtypeStruct((b, s, 3 * ATTN_W), BF16),
            jax.ShapeDtypeStruct((b, s, CONV_W), F32),
            jax.ShapeDtypeStruct((b, s, CONV_W), F32),
        ],
        compiler_params=pltpu.CompilerParams(
            dimension_semantics=("parallel", "parallel"), vmem_limit_bytes=VMEM_LIMIT
        ),
        name="proj_in",
    )(h, g_mix, w_in)


def _bias_tables(rpb):
    depth = rpb.shape[0]
    n_off = 2 * WIN_H - 1
    n_col_off = 2 * WIN_W - 1
    cols = np.arange(GRID_W)
    col_start = np.clip(cols - WIN_W // 2, 0, GRID_W - WIN_W)
    kc = np.arange(GRID_W)
    valid = (kc[None, :] >= col_start[:, None]) & (kc[None, :] < col_start[:, None] + WIN_W)
    off = np.where(valid, kc[None, :] - cols[:, None] + (WIN_W - 1), n_col_off)
    masked_col = jnp.full(rpb.shape[:-1] + (1,), MASK_VALUE, F32)
    colpart = jnp.concatenate([rpb * LOG2E, masked_col], axis=-1)[:, :, :, off]
    colpart = colpart.reshape(depth, N_GROUPS, HEADS_PER_GROUP, n_off, N_SLABS, SLAB, N_SLABS, SLAB)
    colpart = colpart.transpose(0, 1, 6, 4, 2, 5, 3, 7)
    colpart = colpart.reshape(depth, N_GROUPS, N_SLABS, N_SLABS * SLAB_ROWS, n_off, SLAB)
    tiles = [colpart[..., s : s + WIN_H, :].reshape(depth, N_GROUPS, N_SLABS, N_SLABS * SLAB_ROWS, LANES)
             for s in range(WIN_H)]
    return jnp.stack(tiles, axis=3).astype(F32)


_ACTIVE_TILES = (
    ((0, 0), (0, 1)),
    ((0, 0), (0, 1), (1, 0)),
    ((0, 1), (1, 0), (1, 1)),
    ((1, 0), (1, 1)),
)


def _attention_kernel(q_ref, k_ref, v_ref, bias_ref, g_ref, o_ref,
                      sa_scr, sb_scr, pa_scr, pb_scr, l_scr):
    n_rows = k_ref.shape[1] // GRID_W
    lane_head = lax.broadcasted_iota(jnp.int32, (SLAB, GROUP_W), 1) // HEAD_DIM
    s_scr = (sa_scr, sb_scr)
    p_scr = (pa_scr, pb_scr)

    pa_scr[:, :, 2 * SLAB_ROWS :, :LANES] = jnp.zeros((2, N_GROUPS, SLAB_ROWS, LANES), BF16)
    pb_scr[:, :, :SLAB_ROWS, LANES:] = jnp.zeros((2, N_GROUPS, SLAB_ROWS, LANES), BF16)

    def window(r):
        r0 = jnp.clip(r - WIN_H // 2, 0, n_rows - WIN_H)
        return r - r0, r0 * GRID_W

    def half_window(ref, key0, lanes, half):
        win = ref.at[0, pl.ds(pl.multiple_of(key0, GRID_W), WIN_KEYS)]
        chunks = []
        for kb in (2 * half, 2 * half + 1):
            for kk in range(WIN_H):
                start = kk * GRID_W + kb * SLAB
                chunks.append(win[start : start + SLAB, lanes])
        return jnp.concatenate(chunks, axis=0)

    def logits(r, slot):
        dlt, key0 = window(r)
        q_row = q_ref[0, pl.ds(pl.multiple_of(r * GRID_W, GRID_W), GRID_W), :]
        for g in range(N_GROUPS):
            lanes = slice(g * GROUP_W, (g + 1) * GROUP_W)
            q4 = q_row[:, lanes]
            lhs = jnp.concatenate(
                [jnp.where(lane_head == h, q4[a * SLAB : (a + 1) * SLAB], jnp.zeros((SLAB, GROUP_W), BF16))
                 for a in range(N_SLABS) for h in range(HEADS_PER_GROUP)],
                axis=0,
            )
            for half in range(2):
                rows = slice(half * SLAB_ROWS, half * SLAB_ROWS + HALF_ROWS)
                k_half = half_window(k_ref, key0, lanes, half)
                sc = lax.dot_general(lhs[rows], k_half, (((1,), (1,)), ((), ())),
                                     preferred_element_type=F32)
                bias = jnp.concatenate(
                    [bias_ref[g, 2 * half + blk, WIN_H - 1 - dlt, rows, :] for blk in range(2)], axis=1)
                s_scr[half][slot, g] = sc + bias

    def softmax(slot):
        for g in range(N_GROUPS):
            for a in range(N_SLABS):
                tiles_at = [
                    (half, slice((a - half) * SLAB_ROWS, (a - half + 1) * SLAB_ROWS),
                     slice(blk * LANES, (blk + 1) * LANES))
                    for half, blk in _ACTIVE_TILES[a]
                ]
                tiles = [s_scr[half][slot, g, rows, cols] for half, rows, cols in tiles_at]
                m = jnp.max(functools.reduce(jnp.maximum, tiles), axis=-1, keepdims=True)
                es = [jnp.exp2(t - m) for t in tiles]
                l = jnp.sum(functools.reduce(jnp.add, es), axis=-1, keepdims=True)
                l_scr[slot, g, a * SLAB_ROWS : (a + 1) * SLAB_ROWS, :] = jnp.broadcast_to(l, (SLAB_ROWS, LANES))
                for e, (half, rows, cols) in zip(es, tiles_at):
                    p_scr[half][slot, g, rows, cols] = e.astype(BF16)

    def output(r, slot):
        _, key0 = window(r)
        groups = []
        for g in range(N_GROUPS):
            lanes = slice(g * GROUP_W, (g + 1) * GROUP_W)
            oa = jnp.dot(pa_scr[slot, g], half_window(v_ref, key0, lanes, 0), preferred_element_type=F32)
            ob = jnp.dot(pb_scr[slot, g], half_window(v_ref, key0, lanes, 1), preferred_element_type=F32)
            inv_l = 1.0 / l_scr[slot, g]
            slabs = []
            for a in range(N_SLABS):
                parts = []
                if a < N_SLABS - 1:
                    parts.append(oa[a * SLAB_ROWS : (a + 1) * SLAB_ROWS])
                if a > 0:
                    parts.append(ob[(a - 1) * SLAB_ROWS : a * SLAB_ROWS])
                inv = inv_l[a * SLAB_ROWS : (a + 1) * SLAB_ROWS]
                o = functools.reduce(jnp.add, parts) * jnp.concatenate([inv] * (GROUP_W // LANES), axis=1)
                o16 = jnp.zeros((SLAB, GROUP_W), F32)
                for h in range(HEADS_PER_GROUP):
                    o16 = o16 + jnp.where(lane_head == h, o[h * SLAB : (h + 1) * SLAB], 0.0)
                slabs.append(o16)
            groups.append(jnp.concatenate(slabs, axis=0))
        attn = jnp.concatenate(groups, axis=1)
        tok = pl.multiple_of(r * GRID_W, GRID_W)
        o_ref[0, pl.ds(tok, GRID_W), :] = _rms(attn, g_ref[...]).astype(BF16)

    def pair(j, carry):
        i = 2 * j
        logits(i, 0)
        softmax(1)
        output(i - 2, 0)
        logits(i + 1, 1)
        softmax(0)
        output(i - 1, 1)
        return carry

    logits(0, 0)
    logits(1, 1)
    softmax(0)
    lax.fori_loop(1, n_rows // 2, pair, 0, unroll=PAIR_UNROLL)
    softmax(1)
    output(n_rows - 2, 0)
    output(n_rows - 1, 1)


def _attention(qkv, bias_tbl, g_attn_out, layer):
    b, s, _ = qkv.shape
    image = lambda col: pl.BlockSpec((1, s, ATTN_W), lambda i: (i, 0, col))
    table = pl.BlockSpec(
        (None,) + bias_tbl.shape[1:],
        lambda i: (layer, 0, 0, 0, 0, 0),
        pipeline_mode=pl.Buffered(1),
    )
    half_scores = lambda dtype: pltpu.VMEM((2, N_GROUPS, HALF_ROWS, HALF_KEYS), dtype)
    return pl.pallas_call(
        _attention_kernel,
        grid=(b,),
        in_specs=[
            image(0),
            image(1),
            image(2),
            table,
            pl.BlockSpec((None, 1, ATTN_W), lambda i: (layer, 0, 0)),
        ],
        out_specs=image(0),
        out_shape=jax.ShapeDtypeStruct((b, s, ATTN_W), BF16),
        scratch_shapes=[
            half_scores(F32),
            half_scores(F32),
            half_scores(BF16),
            half_scores(BF16),
            pltpu.VMEM((2, N_GROUPS, N_SLABS * SLAB_ROWS, LANES), F32),
        ],
        compiler_params=pltpu.CompilerParams(
            dimension_semantics=("parallel",), vmem_limit_bytes=VMEM_LIMIT
        ),
        name="attention",
    )(qkv, qkv, qkv, bias_tbl, g_attn_out)


def _post_kernel(
    h_ref, attn_ref, cb_ref, u_ref, up_ref, un_ref, p_ref,
    cw_ref, gconv_ref, wout_ref, gffn_ref, wgate_ref, wup_ref, wdown_ref,
    gple_ref, wpg_ref, wpp_ref, gfin_ref, o_ref, *, last,
):
    j = pl.program_id(1)
    nj = pl.num_programs(1)
    tm = h_ref.shape[1]

    u = u_ref[0]
    row = lax.broadcasted_iota(jnp.int32, u.shape, 0)
    prev_row = jnp.where(j == 0, 0.0, up_ref[0, HALO - 1 : HALO, :])
    next_row = jnp.where(j == nj - 1, 0.0, un_ref[0, 0:1, :])
    u_prev = jnp.where(row == 0, prev_row, pltpu.roll(u, 1, 0))
    u_next = jnp.where(row == tm - 1, next_row, pltpu.roll(u, tm - 1, 0))
    conv = cb_ref[0] * (cw_ref[0:1, :] * u_prev + cw_ref[1:2, :] * u + cw_ref[2:3, :] * u_next)
    conv_n = _rms(conv, gconv_ref[...]).astype(BF16)

    mixed = jnp.concatenate([attn_ref[0], conv_n], axis=1)
    h1 = h_ref[0] + jnp.dot(mixed, wout_ref[...], preferred_element_type=F32)

    hn = _rms(h1, gffn_ref[...]).astype(BF16)
    gate = jnp.dot(hn, wgate_ref[...], preferred_element_type=F32)
    up = jnp.dot(hn, wup_ref[...], preferred_element_type=F32)
    act = (gate * jax.nn.sigmoid(gate) * up).astype(BF16)
    h2 = h1 + jnp.dot(act, wdown_ref[...], preferred_element_type=F32)

    hn = _rms(h2, gple_ref[...]).astype(BF16)
    gate = jax.nn.sigmoid(jnp.dot(hn, wpg_ref[...], preferred_element_type=F32))
    proj = jnp.dot(p_ref[0].astype(BF16), wpp_ref[...], preferred_element_type=F32)
    h3 = h2 + gate * proj
    if last:
        h3 = _rms(h3, gfin_ref[...])
    o_ref[0] = h3


def _post(h, attn, cb, u, p, conv_w, g_conv_out, w_out, g_ffn, w_gate, w_up, w_down,
          g_ple, w_ple_gate, w_ple_proj, g_final, layer, last):
    b, s, d = h.shape
    tm = TM_POST
    hb = tm // HALO
    n_halo = s // HALO

    def tile(width):
        return pl.BlockSpec((1, tm, width), lambda i, j: (i, j, 0))

    def resident(shape):
        zeros = (0,) * len(shape)
        return pl.BlockSpec((None,) + shape, lambda i, j: (layer,) + zeros,
                            pipeline_mode=pl.Buffered(1))

    in_specs = [
        tile(d),
        tile(ATTN_W),
        tile(CONV_W),
        tile(CONV_W),
        pl.BlockSpec((1, HALO, CONV_W), lambda i, j: (i, jnp.maximum(j * hb - 1, 0), 0)),
        pl.BlockSpec((1, HALO, CONV_W), lambda i, j: (i, jnp.minimum((j + 1) * hb, n_halo - 1), 0)),
        pl.BlockSpec((None, 1, tm, PLE_DIM), lambda i, j: (layer, i, j, 0)),
        resident((CONV_K, CONV_W)),
        resident((1, CONV_W)),
        resident((d, d)),
        resident((1, d)),
        resident((d, D_FF)),
        resident((d, D_FF)),
        resident((D_FF, d)),
        resident((1, d)),
        resident((d, d)),
        resident((PLE_DIM, d)),
        pl.BlockSpec((1, d), lambda i, j: (0, 0)),
    ]
    return pl.pallas_call(
        functools.partial(_post_kernel, last=last),
        grid=(b, s // tm),
        in_specs=in_specs,
        out_specs=tile(d),
        out_shape=jax.ShapeDtypeStruct((b, s, d), F32),
        compiler_params=pltpu.CompilerParams(
            dimension_semantics=("parallel", "parallel"), vmem_limit_bytes=VMEM_LIMIT
        ),
        name="post",
    )(h, attn, cb, u, u, u, p, conv_w, g_conv_out, w_out, g_ffn, w_gate, w_up, w_down,
      g_ple, w_ple_gate, w_ple_proj, g_final)


def kernel(x, p, g_mix, w_in, rpb, conv_w, g_attn_out, g_conv_out, w_out, g_ffn, w_gate, w_up,
           w_down, g_ple, w_ple_gate, w_ple_proj, g_final):
    depth = w_in.shape[0]
    row = lambda g: g.reshape(g.shape[0], 1, g.shape[1])
    g_mix, g_attn_out, g_conv_out, g_ffn, g_ple = map(row, (g_mix, g_attn_out, g_conv_out, g_ffn, g_ple))
    g_final = g_final.reshape(1, -1)
    w_in, w_out, w_gate, w_up, w_down, w_ple_gate, w_ple_proj = (
        w.astype(BF16) for w in (w_in, w_out, w_gate, w_up, w_down, w_ple_gate, w_ple_proj)
    )
    bias_tbl = _bias_tables(rpb)

    h = x
    for layer in range(depth):
        qkv, cb, u = _proj_in(h, g_mix, w_in, layer)
        attn = _attention(qkv, bias_tbl, g_attn_out, layer)
        h = _post(h, attn, cb, u, p, conv_w, g_conv_out, w_out, g_ffn, w_gate, w_up, w_down,
                  g_ple, w_ple_gate, w_ple_proj, g_final, layer, layer == depth - 1)
    return h
```
